```python
import math
import jax
import jax.numpy as jnp
from jax import lax
import numpy as np

D_MODEL = 2048
BATCH = 4
SEQ = 2048
DEPTH = 4

GRID_W = 64
CTX_LEN = 256
N_MOD = 6
NORM_EPS = 1e-6
LRU_WIDTH = 1024
LRU_BLOCKS = 8
LRU_BLOCK = LRU_WIDTH // LRU_BLOCKS
LRU_C = 8.0
GDN_HEADS = 8
GDN_DK = 128
GDN_DV = 128
QK_WIDTH = GDN_HEADS * GDN_DK
V_WIDTH = GDN_HEADS * GDN_DV
CHUNK = 64
MIX_WIDTH = LRU_WIDTH + V_WIDTH
CONV_W = 4
CONV_PAD_LEFT = 2
CONV_PAD_RIGHT = CONV_W - 1 - CONV_PAD_LEFT
CONV_CH = LRU_WIDTH + 2 * QK_WIDTH + V_WIDTH
IN_WIDTH = CONV_CH + LRU_WIDTH + V_WIDTH + 4 * GDN_HEADS
N_EXPERTS = 64
TOP_K = 8
D_EXPERT = 256
D_SHARED = 256
EXPERT_GROUP = 8
ROUTED_SCALE = 2.5

kernel_name = 'hybrid_rglru_gdn_moe_dit_trunk'


def rmsnorm(x, g):
    xf = x.astype(jnp.float32)
    y = xf * lax.rsqrt(jnp.mean(xf * xf, axis=-1, keepdims=True) + NORM_EPS)
    return y.astype(x.dtype) * g


def l2norm(x):
    return x * lax.rsqrt(jnp.sum(x * x, axis=-1, keepdims=True) + NORM_EPS)


def to_col_major(t, rows):
    b, n, ch = t.shape
    return t.reshape(b, rows, GRID_W, ch).transpose(0, 2, 1, 3).reshape(b, n, ch)


def from_col_major(t, rows):
    b, n, ch = t.shape
    return t.reshape(b, GRID_W, rows, ch).transpose(0, 2, 1, 3).reshape(b, n, ch)


def dwconv_centred(u, w):
    n = u.shape[1]
    up = jnp.pad(u, ((0, 0), (CONV_PAD_LEFT, CONV_PAD_RIGHT), (0, 0)))
    return sum(up[:, j:j + n] * w[j] for j in range(CONV_W))


def linear_scan(a, b, h0, reverse):
    def combine(e1, e2):
        a1, b1 = e1
        a2, b2 = e2
        return a1 * a2, a2 * b1 + b2
    a_cum, b_cum = lax.associative_scan(combine, (a, b), axis=1, reverse=reverse)
    return b_cum + a_cum * h0[:, None]


def rglru_coeffs(u, w_a, b_a, w_i, b_i, lam):
    ub = u.reshape(u.shape[:-1] + (LRU_BLOCKS, LRU_BLOCK))
    r = jax.nn.sigmoid(jnp.einsum('btnc,ncd->btnd', ub, w_a).reshape(u.shape) + b_a)
    i = jax.nn.sigmoid(jnp.einsum('btnc,ncd->btnd', ub, w_i).reshape(u.shape) + b_i)
    log_a = LRU_C * r * jax.nn.log_sigmoid(lam)
    a = jnp.exp(log_a)
    b = jnp.sqrt(jnp.maximum(-jnp.expm1(2.0 * log_a), 0.0)) * (i * u)
    return a, b


def rglru_direction(u, n_ctx, w_a, b_a, w_i, b_i, lam, reverse):
    a, b = rglru_coeffs(u, w_a, b_a, w_i, b_i, lam)
    zero = jnp.zeros((u.shape[0], u.shape[2]), u.dtype)
    h_ctx = linear_scan(a[:, :n_ctx], b[:, :n_ctx], zero, reverse)
    h0 = h_ctx[:, 0] if reverse else h_ctx[:, -1]
    h_lat = linear_scan(a[:, n_ctx:], b[:, n_ctx:], h0, reverse)
    return jnp.concatenate([h_ctx, h_lat], axis=1)


def to_chunks(t, n_chunks):
    b, _, h = t.shape[:3]
    t = t.reshape((b, n_chunks, CHUNK, h) + t.shape[3:])
    return t.transpose((1, 0, 3, 2) + tuple(range(4, t.ndim)))


def gdn_chunked(q, k, v, g, beta, s0):
    b, n, h, _ = q.shape
    nc = n // CHUNK
    qc, kc, vc = to_chunks(q, nc), to_chunks(k, nc), to_chunks(v, nc)
    gc, bc = to_chunks(g, nc), to_chunks(beta, nc)
    cum = jnp.cumsum(gc, axis=-1)
    pos = jnp.arange(CHUNK)
    incl = pos[:, None] >= pos[None, :]
    strict = pos[:, None] > pos[None, :]
    gamma = jnp.exp(jnp.where(incl, cum[..., :, None] - cum[..., None, :], -jnp.inf))
    kk = jnp.einsum('nbhid,nbhjd->nbhij', kc, kc)
    a_low = jnp.where(strict, bc[..., :, None] * kk * gamma, 0.0)
    eye = jnp.eye(CHUNK, dtype=a_low.dtype)
    rhs = jnp.concatenate([bc[..., None] * vc, bc[..., None] * kc * jnp.exp(cum)[..., None]], axis=-1)
    sol = lax.linalg.triangular_solve(a_low + eye, rhs, left_side=True, lower=True, unit_diagonal=True)
    u_c, w_c = sol[..., :GDN_DV], sol[..., GDN_DV:]
    qk = jnp.einsum('nbhid,nbhjd->nbhij', qc, kc) * gamma
    q_dec = qc * jnp.exp(cum)[..., None]
    k_dec = kc * jnp.exp(cum[..., -1:] - cum)[..., None]
    d_last = jnp.exp(cum[..., -1])

    def step(s, xs):
        u_n, w_n, qk_n, qd_n, kd_n, dl_n = xs
        v_new = u_n - jnp.einsum('bhcd,bhde->bhce', w_n, s)
        o = jnp.einsum('bhcd,bhde->bhce', qd_n, s) + jnp.einsum('bhij,bhje->bhie', qk_n, v_new)
        s = dl_n[..., None, None] * s + jnp.einsum('bhcd,bhce->bhde', kd_n, v_new)
        return s, o

    s_fin, o = lax.scan(step, s0, (u_c, w_c, qk, q_dec, k_dec, d_last))
    o = o.transpose(1, 0, 3, 2, 4).reshape(b, n, h, GDN_DV)
    return o, s_fin


def gdn_direction(q, k, v, g, beta, n_ctx, reverse):
    b = q.shape[0]
    s = jnp.zeros((b, GDN_HEADS, GDN_DK, GDN_DV), q.dtype)

    def run(sl, s_init):
        args = [t[:, sl] for t in (q, k, v, g, beta)]
        if reverse:
            args = [jnp.flip(t, axis=1) for t in args]
        o, s_out = gdn_chunked(*args, s_init)
        if reverse:
            o = jnp.flip(o, axis=1)
        return o, s_out

    o_ctx, s = run(slice(0, n_ctx), s)
    o_lat, _ = run(slice(n_ctx, None), s)
    return jnp.concatenate([o_ctx, o_lat], axis=1)


def mixer_features(h, n_ctx, w_in, conv_w, lru_w_a, lru_b_a, lru_w_i, lru_b_i, lru_lam,
                   gdn_a_log, gdn_dt_bias, gdn_norm_g):
    b, t, _ = h.shape
    p = (h @ w_in).astype(jnp.float32)
    pc, pr = p[..., :CONV_CH], p[..., CONV_CH:]
    u = jnp.concatenate([dwconv_centred(pc[:, :n_ctx], conv_w),
                         dwconv_centred(pc[:, n_ctx:], conv_w)], axis=1)
    o1 = LRU_WIDTH
    o2 = o1 + QK_WIDTH
    o3 = o2 + QK_WIDTH
    lru_x = u[..., :o1]
    q = jax.nn.silu(u[..., o1:o2])
    k = jax.nn.silu(u[..., o2:o3])
    v = jax.nn.silu(u[..., o3:])
    r1 = LRU_WIDTH
    r2 = r1 + V_WIDTH
    r3 = r2 + 2 * GDN_HEADS
    lru_y, z, b_raw, a_raw = pr[..., :r1], pr[..., r1:r2], pr[..., r2:r3], pr[..., r3:]
    h_lru = sum(rglru_direction(lru_x, n_ctx, lru_w_a[d], lru_b_a[d], lru_w_i[d], lru_b_i[d],
                                lru_lam[d], reverse=bool(d)) for d in range(2))
    y_lru = jax.nn.gelu(lru_y) * h_lru
    q = l2norm(q.reshape(b, t, GDN_HEADS, GDN_DK)) * (GDN_DK ** -0.5)
    k = l2norm(k.reshape(b, t, GDN_HEADS, GDN_DK))
    v = v.reshape(b, t, GDN_HEADS, GDN_DV)
    beta = jax.nn.sigmoid(b_raw).reshape(b, t, 2, GDN_HEADS)
    g = -jnp.exp(gdn_a_log) * jax.nn.softplus(a_raw.reshape(b, t, 2, GDN_HEADS) + gdn_dt_bias)
    o = sum(gdn_direction(q, k, v, g[:, :, d], beta[:, :, d], n_ctx, reverse=bool(d)) for d in range(2))
    y_gdn = rmsnorm(o, gdn_norm_g) * jax.nn.silu(z.reshape(b, t, GDN_HEADS, GDN_DV))
    return jnp.concatenate([y_lru, y_gdn.reshape(b, t, V_WIDTH)], axis=-1)


def swiglu(t, w_g, w_u, w_d):
    return (jax.nn.silu(t @ w_g) * (t @ w_u)) @ w_d


def moe_ffn(h, w_router, router_bias, w_gate, w_up, w_down, ws_gate, ws_up, ws_down):
    shape = h.shape
    t = h.reshape(-1, shape[-1])
    scores = jax.nn.sigmoid((t @ w_router).astype(jnp.float32))
    _, idx = lax.top_k(scores + router_bias.astype(jnp.float32), TOP_K)
    sel = jnp.take_along_axis(scores, idx, axis=-1)
    wts = sel / jnp.sum(sel, axis=-1, keepdims=True) * ROUTED_SCALE
    gates = jnp.einsum('tk,tke->te', wts, jax.nn.one_hot(idx, N_EXPERTS, dtype=jnp.float32)).astype(t.dtype)
    y = swiglu(t, ws_gate, ws_up, ws_down)
    for e0 in range(0, N_EXPERTS, EXPERT_GROUP):
        sl = slice(e0, e0 + EXPERT_GROUP)
        hid = jax.nn.silu(jnp.einsum('td,edf->tef', t, w_gate[sl])) * jnp.einsum('td,edf->tef', t, w_up[sl])
        y = y + jnp.einsum('tef,efd->td', hid * gates[:, sl, None], w_down[sl])
    return y.reshape(shape)


def setup_inputs(seed: int = 0) -> dict:
    key = jax.random.key(seed)
    ks = iter(jax.random.split(key, 32))
    f32 = jnp.float32
    D = D_MODEL

    def nrm(shape, scale):
        return jax.random.normal(next(ks), shape, f32) * scale

    x = nrm((BATCH, SEQ, D), 1.0)
    c = nrm((BATCH, D), 1.0)
    ctx = nrm((BATCH, CTX_LEN, D), 1.0)
    c_ctx = nrm((D,), 1.0)
    w_mod = nrm((DEPTH, D, N_MOD * D), 0.5 * D ** -0.5)
    b_mod = nrm((DEPTH, N_MOD * D), 0.02)
    g_mix = 1.0 + nrm((DEPTH, D), 0.02)
    g_ffn = 1.0 + nrm((DEPTH, D), 0.02)
    w_in = nrm((DEPTH, D, IN_WIDTH), D ** -0.5)
    conv_w = nrm((DEPTH, CONV_W, CONV_CH), CONV_W ** -0.5)
    lru_w_a = nrm((DEPTH, 2, LRU_BLOCKS, LRU_BLOCK, LRU_BLOCK), LRU_BLOCK ** -0.5)
    lru_b_a = nrm((DEPTH, 2, LRU_WIDTH), 0.1)
    lru_w_i = nrm((DEPTH, 2, LRU_BLOCKS, LRU_BLOCK, LRU_BLOCK), LRU_BLOCK ** -0.5)
    lru_b_i = nrm((DEPTH, 2, LRU_WIDTH), 0.1)
    a_pow = jax.random.uniform(next(ks), (DEPTH, 2, LRU_WIDTH), f32, 0.9, 0.999)
    p_sig = a_pow ** (1.0 / LRU_C)
    lru_lam = jnp.log(p_sig) - jnp.log1p(-p_sig)
    gdn_a_log = jnp.log(jax.random.uniform(next(ks), (DEPTH, 2, GDN_HEADS), f32, 1.0, 16.0))
    dt = jnp.exp(jax.random.uniform(next(ks), (DEPTH, 2, GDN_HEADS), f32, math.log(1e-3), math.log(1e-1)))
    gdn_dt_bias = dt + jnp.log(-jnp.expm1(-dt))
    gdn_norm_g = 1.0 + nrm((DEPTH, GDN_DV), 0.02)
    w_out = nrm((DEPTH, MIX_WIDTH, D), MIX_WIDTH ** -0.5)
    w_router = nrm((DEPTH, D, N_EXPERTS), D ** -0.5)
    router_bias = nrm((DEPTH, N_EXPERTS), 0.01)
    w_gate = nrm((DEPTH, N_EXPERTS, D, D_EXPERT), D ** -0.5)
    w_up = nrm((DEPTH, N_EXPERTS, D, D_EXPERT), D ** -0.5)
    w_down = nrm((DEPTH, N_EXPERTS, D_EXPERT, D), D_EXPERT ** -0.5)
    ws_gate = nrm((DEPTH, D, D_SHARED), D ** -0.5)
    ws_up = nrm((DEPTH, D, D_SHARED), D ** -0.5)
    ws_down = nrm((DEPTH, D_SHARED, D), D_SHARED ** -0.5)
    g_final = 1.0 + nrm((D,), 0.02)
    return {'x': x, 'c': c, 'ctx': ctx, 'c_ctx': c_ctx, 'w_mod': w_mod, 'b_mod': b_mod,
            'g_mix': g_mix, 'g_ffn': g_ffn, 'w_in': w_in, 'conv_w': conv_w,
            'lru_w_a': lru_w_a, 'lru_b_a': lru_b_a, 'lru_w_i': lru_w_i, 'lru_b_i': lru_b_i,
            'lru_lam': lru_lam, 'gdn_a_log': gdn_a_log, 'gdn_dt_bias': gdn_dt_bias,
            'gdn_norm_g': gdn_norm_g, 'w_out': w_out, 'w_router': w_router,
            'router_bias': router_bias, 'w_gate': w_gate, 'w_up': w_up, 'w_down': w_down,
            'ws_gate': ws_gate, 'ws_up': ws_up, 'ws_down': ws_down, 'g_final': g_final}


def reference(x, c, ctx, c_ctx, w_mod, b_mod, g_mix, g_ffn, w_in, conv_w, lru_w_a, lru_b_a,
              lru_w_i, lru_b_i, lru_lam, gdn_a_log, gdn_dt_bias, gdn_norm_g, w_out, w_router,
              router_bias, w_gate, w_up, w_down, ws_gate, ws_up, ws_down, g_final):
    n_ctx = ctx.shape[1]
    rows = x.shape[1] // GRID_W
    xl, xc = x, ctx
    silu_c = jax.nn.silu(c)
    silu_cc = jax.nn.silu(c_ctx)
    for l in range(DEPTH):
        last = l == DEPTH - 1
        m_l = jnp.split(silu_c @ w_mod[l] + b_mod[l], N_MOD, axis=-1)
        m_c = jnp.split(silu_cc @ w_mod[l] + b_mod[l], N_MOD, axis=-1)
        hl = rmsnorm(xl, g_mix[l]) * (1.0 + m_l[1][:, None]) + m_l[0][:, None]
        hc = rmsnorm(xc, g_mix[l]) * (1.0 + m_c[1]) + m_c[0]
        col_major = l % 2 == 1
        if col_major:
            hl = to_col_major(hl, rows)
        feats = mixer_features(jnp.concatenate([hc, hl], axis=1), n_ctx, w_in[l], conv_w[l],
                               lru_w_a[l], lru_b_a[l], lru_w_i[l], lru_b_i[l], lru_lam[l],
                               gdn_a_log[l], gdn_dt_bias[l], gdn_norm_g[l]).astype(x.dtype)
        yl = feats[:, n_ctx:] @ w_out[l]
        if col_major:
            yl = from_col_major(yl, rows)
        xl = xl + m_l[2][:, None] * yl
        if not last:
            xc = xc + m_c[2] * (feats[:, :n_ctx] @ w_out[l])
        hl = rmsnorm(xl, g_ffn[l]) * (1.0 + m_l[4][:, None]) + m_l[3][:, None]
        moe_args = (w_router[l], router_bias[l], w_gate[l], w_up[l], w_down[l],
                    ws_gate[l], ws_up[l], ws_down[l])
        if last:
            xl = xl + m_l[5][:, None] * moe_ffn(hl, *moe_args)
        else:
            hc = rmsnorm(xc, g_ffn[l]) * (1.0 + m_c[4]) + m_c[3]
            y = moe_ffn(jnp.concatenate([hc, hl], axis=1), *moe_args)
            xc = xc + m_c[5] * y[:, :n_ctx]
            xl = xl + m_l[5][:, None] * y[:, n_ctx:]
    return rmsnorm(xl, g_final)
```

```python
import functools

import jax
import jax.numpy as jnp
from jax import lax
from jax.experimental import pallas as pl
from jax.experimental.pallas import tpu as pltpu

F32 = jnp.float32
BF16 = jnp.bfloat16
HI = lax.Precision.HIGHEST

GRID_W = 64
N_MOD = 6
NORM_EPS = 1e-6
LRU_WIDTH = 1024
LRU_BLOCK = 128
LRU_C = 8.0
GDN_HEADS = 8
GDN_DK = 128
GDN_DV = 128
QK_WIDTH = GDN_HEADS * GDN_DK
V_WIDTH = GDN_HEADS * GDN_DV
CHUNK = 64
CONV_W = 4
CONV_PAD_LEFT = 2
CONV_CH = LRU_WIDTH + 2 * QK_WIDTH + V_WIDTH
MAIN_WIDTH = CONV_CH + LRU_WIDTH + V_WIDTH
N_GATE = 4 * GDN_HEADS
TOP_K = 8
ROUTED_SCALE = 2.5

LANE = 128
ROW_TILE = 256
VMEM_LIMIT = 56 * 1024 * 1024

NT_DIMS = (((1,), (1,)), ((), ()))
TN_DIMS = (((0,), (0,)), ((), ()))


def _params(*sem):
    return pltpu.CompilerParams(dimension_semantics=sem, vmem_limit_bytes=VMEM_LIMIT)


def _silu(x):
    return x * jax.nn.sigmoid(x)


def _norm_mod(x, g, shift, scale):
    ms = jnp.mean(x * x, axis=-1, keepdims=True)
    return (x * lax.rsqrt(ms + NORM_EPS) * g) * (1.0 + scale) + shift


def _mod_kernel(s_ref, w_ref, b_ref, o_ref):
    s = _silu(s_ref[...])
    o_ref[...] = jnp.dot(s.astype(BF16), w_ref[...].astype(BF16),
                         preferred_element_type=F32) + b_ref[...]


def _modulation(c, c_ctx, w_mod, b_mod):
    depth, d, nd = w_mod.shape
    b = c.shape[0]
    rows = jnp.concatenate([c, c_ctx[None], jnp.zeros((8 - b - 1, d), F32)], axis=0)
    tn = 1024
    out = pl.pallas_call(
        _mod_kernel,
        grid=(depth, nd // tn),
        in_specs=[pl.BlockSpec((8, d), lambda l, j: (0, 0)),
                  pl.BlockSpec((None, d, tn), lambda l, j: (l, 0, j)),
                  pl.BlockSpec((None, 1, tn), lambda l, j: (l, 0, j))],
        out_specs=pl.BlockSpec((None, 8, tn), lambda l, j: (l, 0, j)),
        out_shape=jax.ShapeDtypeStruct((depth, 8, nd), F32),
        compiler_params=_params("arbitrary", "arbitrary"),
    )(rows, w_mod, b_mod.reshape(depth, 1, nd))
    return out.reshape(depth, 8, N_MOD, d)


def _inprep_kernel(x_ref, mod_ref, g_ref, wg_ref, h_ref, pg_ref):
    h = _norm_mod(x_ref[...], g_ref[...], mod_ref[0:1, :], mod_ref[1:2, :])
    hb = h.astype(BF16)
    h_ref[...] = hb
    pg_ref[...] = jnp.dot(hb, wg_ref[...], preferred_element_type=F32)


def _inprep(x, mods, g, w_gatecols, l, n_ctx):
    b, t, d = x.shape
    tm = ROW_TILE
    nct = n_ctx // tm
    return pl.pallas_call(
        _inprep_kernel,
        grid=(b, t // tm),
        in_specs=[pl.BlockSpec((None, tm, d), lambda bi, i: (bi, i, 0)),
                  pl.BlockSpec((None, None, N_MOD, d),
                               lambda bi, i: (l, jnp.where(i < nct, b, bi), 0, 0)),
                  pl.BlockSpec((None, 1, d), lambda bi, i: (l, 0, 0)),
                  pl.BlockSpec((d, N_GATE), lambda bi, i: (0, 0))],
        out_specs=[pl.BlockSpec((None, tm, d), lambda bi, i: (bi, i, 0)),
                   pl.BlockSpec((None, tm, N_GATE), lambda bi, i: (bi, i, 0))],
        out_shape=[jax.ShapeDtypeStruct((b, t, d), BF16),
                   jax.ShapeDtypeStruct((b, t, N_GATE), F32)],
        compiler_params=_params("arbitrary", "arbitrary"),
    )(x, mods, g, w_gatecols)


def _matmul_kernel(h_ref, w_ref, o_ref, wb_ref):
    @pl.when(pl.program_id(1) == 0)
    def _():
        wb_ref[...] = w_ref[...].astype(BF16)

    o_ref[...] = jnp.dot(h_ref[...], wb_ref[...], preferred_element_type=F32)


def _inproj(h2d, w_in, l):
    n, d = h2d.shape
    tn, tm = 1024, 512
    return pl.pallas_call(
        _matmul_kernel,
        grid=(MAIN_WIDTH // tn, n // tm),
        in_specs=[pl.BlockSpec((tm, d), lambda j, i: (i, 0)),
                  pl.BlockSpec((None, d, tn), lambda j, i: (l, 0, j))],
        out_specs=pl.BlockSpec((tm, tn), lambda j, i: (i, j)),
        out_shape=jax.ShapeDtypeStruct((n, MAIN_WIDTH), F32),
        scratch_shapes=[pltpu.VMEM((d, tn), BF16)],
        compiler_params=_params("arbitrary", "arbitrary"),
    )(h2d, w_in)


def _conv_kernel(p_ref, cw_ref, u_ref, *, n_ctx):
    kind = pl.program_id(1) // GDN_HEADS
    x = p_ref[...]
    t_len = x.shape[0]
    t = lax.broadcasted_iota(jnp.int32, x.shape, 0)
    seg_t = jnp.where(t >= n_ctx, 1, 0)
    acc = jnp.zeros_like(x)
    for j in range(CONV_W):
        off = j - CONV_PAD_LEFT
        if off == 0:
            xs = x
        else:
            xs = pltpu.roll(x, (-off) % t_len, 0)
            s = t + off
            seg_s = jnp.where(s >= n_ctx, 1, 0) + jnp.where(s >= t_len, 1, 0) - jnp.where(s < 0, 1, 0)
            xs = jnp.where(seg_s == seg_t, xs, 0.0)
        acc = acc + xs * cw_ref[j:j + 1, :]

    @pl.when(kind == 0)
    def _():
        u_ref[...] = acc

    @pl.when(kind == 3)
    def _():
        u_ref[...] = _silu(acc)

    @pl.when(jnp.logical_or(kind == 1, kind == 2))
    def _():
        y = _silu(acc)
        nrm = y * lax.rsqrt(jnp.sum(y * y, axis=-1, keepdims=True) + NORM_EPS)
        u_ref[...] = nrm * jnp.where(kind == 1, GDN_DK ** -0.5, 1.0)


def _conv(p, conv_w, l, n_ctx):
    b, t, _ = p.shape
    return pl.pallas_call(
        functools.partial(_conv_kernel, n_ctx=n_ctx),
        grid=(b, CONV_CH // LANE),
        in_specs=[pl.BlockSpec((None, t, LANE), lambda bi, j: (bi, 0, j)),
                  pl.BlockSpec((None, CONV_W, LANE), lambda bi, j: (l, 0, j))],
        out_specs=pl.BlockSpec((None, t, LANE), lambda bi, j: (bi, 0, j)),
        out_shape=jax.ShapeDtypeStruct((b, t, CONV_CH), F32),
        compiler_params=_params("arbitrary", "arbitrary"),
    )(p, conv_w)


LRU_LANES = 256


def _lru_kernel(u_ref, y_ref, wa_ref, ba_ref, wi_ref, bi_ref, lam_ref, o_ref,
                af_ref, bf_ref, ab_ref, bb_ref, *, n_ctx):
    t_len, width = u_ref.shape
    nblk = width // LRU_BLOCK
    tm = ROW_TILE

    def gates(i, carry):
        r0 = pl.multiple_of(i * tm, tm)
        u = u_ref[pl.ds(r0, tm), :]
        ub = u.astype(BF16)
        for d, (a_ref, b_ref) in enumerate(((af_ref, bf_ref), (ab_ref, bb_ref))):
            pre_a = jnp.concatenate(
                [jnp.dot(ub[:, n * LRU_BLOCK:(n + 1) * LRU_BLOCK], wa_ref[d, n].astype(BF16),
                         preferred_element_type=F32) for n in range(nblk)], axis=-1)
            pre_i = jnp.concatenate(
                [jnp.dot(ub[:, n * LRU_BLOCK:(n + 1) * LRU_BLOCK], wi_ref[d, n].astype(BF16),
                         preferred_element_type=F32) for n in range(nblk)], axis=-1)
            r = jax.nn.sigmoid(pre_a + ba_ref[d:d + 1, :])
            gi = jax.nn.sigmoid(pre_i + bi_ref[d:d + 1, :])
            lam = lam_ref[d:d + 1, :]
            log_sig = jnp.minimum(lam, 0.0) - jnp.log(1.0 + jnp.exp(-jnp.abs(lam)))
            log_a = LRU_C * r * log_sig
            a = jnp.exp(log_a)
            bcoef = jnp.sqrt(jnp.maximum(1.0 - jnp.exp(2.0 * log_a), 0.0)) * (gi * u)
            a_ref[pl.ds(r0, tm), :] = a
            b_ref[pl.ds(r0, tm), :] = bcoef
        return carry

    lax.fori_loop(0, t_len // tm, gates, 0)

    def scan(gi, carry):
        hf, hb = carry
        base = pl.multiple_of(gi * 8, 8)
        bbase = jnp.where(base < n_ctx, n_ctx - 1 - base, t_len + n_ctx - 1 - base)
        for j in range(8):
            tf = base + j
            tb = bbase - j
            hf = af_ref[pl.ds(tf, 1), :] * hf + bf_ref[pl.ds(tf, 1), :]
            af_ref[pl.ds(tf, 1), :] = hf
            hb = ab_ref[pl.ds(tb, 1), :] * hb + bb_ref[pl.ds(tb, 1), :]
            ab_ref[pl.ds(tb, 1), :] = hb
        return hf, hb

    zero = jnp.zeros((1, width), F32)
    lax.fori_loop(0, t_len // 8, scan, (zero, zero))
    o_ref[...] = jax.nn.gelu(y_ref[...], approximate=True) * (af_ref[...] + ab_ref[...])


def _lru(u, p, lru_w_a, lru_b_a, lru_w_i, lru_b_i, lru_lam, l, n_ctx):
    b, t, _ = u.shape
    w = LRU_LANES
    nb = w // LRU_BLOCK
    y_off = CONV_CH // w
    wspec = pl.BlockSpec((None, 2, nb, LRU_BLOCK, LRU_BLOCK), lambda bi, j: (l, 0, j, 0, 0))
    vspec = pl.BlockSpec((None, 2, w), lambda bi, j: (l, 0, j))
    return pl.pallas_call(
        functools.partial(_lru_kernel, n_ctx=n_ctx),
        grid=(b, LRU_WIDTH // w),
        in_specs=[pl.BlockSpec((None, t, w), lambda bi, j: (bi, 0, j)),
                  pl.BlockSpec((None, t, w), lambda bi, j: (bi, 0, y_off + j)),
                  wspec, vspec, wspec, vspec, vspec],
        out_specs=pl.BlockSpec((None, t, w), lambda bi, j: (bi, 0, j)),
        out_shape=jax.ShapeDtypeStruct((b, t, LRU_WIDTH), F32),
        scratch_shapes=[pltpu.VMEM((t, w), F32)] * 4,
        compiler_params=_params("arbitrary", "arbitrary"),
    )(u, p, lru_w_a, lru_b_a, lru_w_i, lru_b_i, lru_lam)


def _gdn_kernel(q_ref, k_ref, v_ref, gt_ref, alog_ref, dtb_ref, o_ref, s_ref, *, reverse):
    c, h_n, dk, dv = CHUNK, GDN_HEADS, GDN_DK, GDN_DV
    d = int(reverse)

    @pl.when(pl.program_id(1) == 0)
    def _():
        s_ref[...] = jnp.zeros_like(s_ref)

    gt = gt_ref[...]
    beta = jax.nn.sigmoid(gt[:, d * h_n:(d + 1) * h_n])
    xa = gt[:, (2 + d) * h_n:(3 + d) * h_n] + dtb_ref[d:d + 1, :]
    softplus = jnp.maximum(xa, 0.0) + jnp.log(1.0 + jnp.exp(-jnp.abs(xa)))
    g = -jnp.exp(alog_ref[d:d + 1, :]) * softplus

    ri = lax.broadcasted_iota(jnp.int32, (c, c), 0)
    ci = lax.broadcasted_iota(jnp.int32, (c, c), 1)
    incl = (ri <= ci) if reverse else (ri >= ci)
    strict = (ri < ci) if reverse else (ri > ci)
    ones_incl = jnp.where(incl, 1.0, 0.0)
    cum = jnp.dot(ones_incl, g, precision=HI, preferred_element_type=F32)
    cum_t = lax.dot_general(g, ones_incl, (((0,), (1,)), ((), ())), precision=HI,
                            preferred_element_type=F32)
    total = cum[0:1, :] if reverse else cum[c - 1:c, :]
    e_cum = jnp.exp(cum)
    e_rest = jnp.exp(total - cum)
    e_total = jnp.exp(total)

    for h in range(h_n):
        sl = slice(h * dk, (h + 1) * dk)
        q = q_ref[:, sl]
        k = k_ref[:, sl]
        v = v_ref[:, sl]
        qb = q.astype(BF16)
        kb = k.astype(BF16)
        diff = jnp.where(incl, cum[:, h:h + 1] - cum_t[h:h + 1, :], 0.0)
        gamma = jnp.where(incl, jnp.exp(diff), 0.0)
        bcol = beta[:, h:h + 1]
        kk = lax.dot_general(kb, kb, NT_DIMS, preferred_element_type=F32)
        neg_a = jnp.where(strict, -(bcol * kk * gamma), 0.0)
        x = jnp.concatenate([bcol * v, bcol * k * e_cum[:, h:h + 1]], axis=-1)
        x = x + jnp.dot(neg_a, x, precision=HI, preferred_element_type=F32)
        pw = neg_a
        for _ in range(5):
            pw = jnp.dot(pw, pw, precision=HI, preferred_element_type=F32)
            x = x + jnp.dot(pw, x, precision=HI, preferred_element_type=F32)
        u = x[:, :dv]
        w = x[:, dv:]
        qk = lax.dot_general(qb, kb, NT_DIMS, preferred_element_type=F32) * gamma
        q_dec = q * e_cum[:, h:h + 1]
        k_dec = k * e_rest[:, h:h + 1]
        s = s_ref[h]
        ws = jnp.dot(jnp.concatenate([w, q_dec], axis=0).astype(BF16), s.astype(BF16),
                     preferred_element_type=F32)
        v_new = u - ws[:c]
        vb = v_new.astype(BF16)
        o_ref[:, h * dv:(h + 1) * dv] = ws[c:] + jnp.dot(qk.astype(BF16), vb,
                                                         preferred_element_type=F32)
        s_ref[h] = e_total[:, h:h + 1] * s + lax.dot_general(
            k_dec.astype(BF16), vb, TN_DIMS, preferred_element_type=F32)


def _gdn(u, pg, gdn_a_log, gdn_dt_bias, l, n_ctx, reverse):
    b, t, _ = u.shape
    nc = t // CHUNK
    ncc = n_ctx // CHUNK

    def chunk(s):
        if not reverse:
            return s
        return jnp.where(s < ncc, ncc - 1 - s, nc + ncc - 1 - s)

    def qkv_spec(col):
        return pl.BlockSpec((None, CHUNK, QK_WIDTH), lambda bi, s: (bi, chunk(s), col))

    small = pl.BlockSpec((None, 2, GDN_HEADS), lambda bi, s: (l, 0, 0))
    return pl.pallas_call(
        functools.partial(_gdn_kernel, reverse=reverse),
        grid=(b, nc),
        in_specs=[qkv_spec(1), qkv_spec(2), qkv_spec(3),
                  pl.BlockSpec((None, CHUNK, N_GATE), lambda bi, s: (bi, chunk(s), 0)),
                  small, small],
        out_specs=pl.BlockSpec((None, CHUNK, V_WIDTH), lambda bi, s: (bi, chunk(s), 0)),
        out_shape=jax.ShapeDtypeStruct((b, t, V_WIDTH), F32),
        scratch_shapes=[pltpu.VMEM((GDN_HEADS, GDN_DK, GDN_DV), F32)],
        compiler_params=_params("arbitrary", "arbitrary"),
    )(u, u, u, pg, gdn_a_log, gdn_dt_bias)


def _outproj_kernel(ylru_ref, of_ref, ob_ref, z_ref, x_ref, mod_ref, gn_ref, w_ref, o_ref):
    o = of_ref[...] + ob_ref[...]
    z = z_ref[...]
    parts = [ylru_ref[...]]
    for h in range(GDN_HEADS):
        sl = slice(h * GDN_DV, (h + 1) * GDN_DV)
        oh = o[:, sl]
        ms = jnp.mean(oh * oh, axis=-1, keepdims=True)
        parts.append(oh * lax.rsqrt(ms + NORM_EPS) * gn_ref[...] * _silu(z[:, sl]))
    feats = jnp.concatenate(parts, axis=-1).astype(BF16)
    y = jnp.dot(feats, w_ref[...], preferred_element_type=F32)
    o_ref[...] = x_ref[...] + mod_ref[2:3, :] * y


def _outproj(y_lru, o_f, o_b, p, x, mods, gdn_norm_g, w_out_bf, l, n_ctx):
    b, t, d = x.shape
    tm = ROW_TILE
    nct = n_ctx // tm
    z_off = (CONV_CH + LRU_WIDTH) // V_WIDTH
    row = lambda bi, i: (bi, i, 0)
    return pl.pallas_call(
        _outproj_kernel,
        grid=(b, t // tm),
        in_specs=[pl.BlockSpec((None, tm, LRU_WIDTH), row),
                  pl.BlockSpec((None, tm, V_WIDTH), row),
                  pl.BlockSpec((None, tm, V_WIDTH), row),
                  pl.BlockSpec((None, tm, V_WIDTH), lambda bi, i: (bi, i, z_off)),
                  pl.BlockSpec((None, tm, d), row),
                  pl.BlockSpec((None, None, N_MOD, d),
                               lambda bi, i: (l, jnp.where(i < nct, b, bi), 0, 0)),
                  pl.BlockSpec((None, 1, GDN_DV), lambda bi, i: (l, 0, 0)),
                  pl.BlockSpec((None, LRU_WIDTH + V_WIDTH, d), lambda bi, i: (l, 0, 0))],
        out_specs=pl.BlockSpec((None, tm, d), row),
        out_shape=jax.ShapeDtypeStruct((b, t, d), F32),
        compiler_params=_params("arbitrary", "arbitrary"),
    )(y_lru, o_f, o_b, p, x, mods, gdn_norm_g, w_out_bf)


def _router_kernel(x_ref, mod_ref, g_ref, wr_ref, rb_ref, h_ref, idx_ref, wt_ref, rank_ref,
                   cnt_ref, run_ref):
    first = jnp.logical_and(pl.program_id(0) == 0, pl.program_id(1) == 0)

    @pl.when(first)
    def _():
        run_ref[...] = jnp.zeros_like(run_ref)

    h = _norm_mod(x_ref[...], g_ref[...], mod_ref[3:4, :], mod_ref[4:5, :])
    h_ref[...] = h
    scores = jax.nn.sigmoid(jnp.dot(h, wr_ref[...], precision=HI, preferred_element_type=F32))
    tm, n_e = scores.shape
    lane = lax.broadcasted_iota(jnp.int32, (tm, n_e), 1)
    kl = lax.broadcasted_iota(jnp.int32, (tm, TOP_K), 1)
    biased = scores + rb_ref[...]
    idx = jnp.zeros((tm, TOP_K), jnp.int32)
    sel = jnp.zeros((tm, TOP_K), F32)
    maskf = jnp.zeros((tm, n_e), F32)
    onehots = []
    for k in range(TOP_K):
        m = jnp.max(biased, axis=-1, keepdims=True)
        ik = jnp.min(jnp.where(biased == m, lane, n_e), axis=-1, keepdims=True)
        oh = lane == ik
        onehots.append(oh)
        sk = jnp.sum(jnp.where(oh, scores, 0.0), axis=-1, keepdims=True)
        idx = jnp.where(kl == k, ik, idx)
        sel = jnp.where(kl == k, sk, sel)
        maskf = jnp.where(oh, 1.0, maskf)
        biased = jnp.where(oh, -jnp.inf, biased)
    wt_ref[...] = sel / jnp.sum(sel, axis=-1, keepdims=True) * ROUTED_SCALE
    idx_ref[...] = idx

    ri = lax.broadcasted_iota(jnp.int32, (tm, tm), 0)
    ci = lax.broadcasted_iota(jnp.int32, (tm, tm), 1)
    before = jnp.where(ri > ci, 1.0, 0.0).astype(BF16)
    prior = jnp.dot(before, maskf.astype(BF16), preferred_element_type=F32) + run_ref[...]
    rank = jnp.zeros((tm, TOP_K), F32)
    for k in range(TOP_K):
        rk = jnp.sum(jnp.where(onehots[k], prior, 0.0), axis=-1, keepdims=True)
        rank = jnp.where(kl == k, rk, rank)
    rank_ref[...] = rank.astype(jnp.int32)
    run_ref[...] = run_ref[...] + jnp.sum(maskf, axis=0, keepdims=True)
    cnt_ref[...] = run_ref[...]


def _router(x, mods, g, w_router, router_bias, l, n_ctx):
    b, t, d = x.shape
    n_e = w_router.shape[-1]
    tm = ROW_TILE
    nct = n_ctx // tm
    row = lambda bi, i: (bi, i, 0)
    return pl.pallas_call(
        _router_kernel,
        grid=(b, t // tm),
        in_specs=[pl.BlockSpec((None, tm, d), row),
                  pl.BlockSpec((None, None, N_MOD, d),
                               lambda bi, i: (l, jnp.where(i < nct, b, bi), 0, 0)),
                  pl.BlockSpec((None, 1, d), lambda bi, i: (l, 0, 0)),
                  pl.BlockSpec((None, d, n_e), lambda bi, i: (l, 0, 0)),
                  pl.BlockSpec((None, 1, n_e), lambda bi, i: (l, 0, 0))],
        out_specs=[pl.BlockSpec((None, tm, d), row),
                   pl.BlockSpec((None, tm, TOP_K), row),
                   pl.BlockSpec((None, tm, TOP_K), row),
                   pl.BlockSpec((None, tm, TOP_K), row),
                   pl.BlockSpec((1, n_e), lambda bi, i: (0, 0))],
        out_shape=[jax.ShapeDtypeStruct((b, t, d), F32),
                   jax.ShapeDtypeStruct((b, t, TOP_K), jnp.int32),
                   jax.ShapeDtypeStruct((b, t, TOP_K), F32),
                   jax.ShapeDtypeStruct((b, t, TOP_K), jnp.int32),
                   jax.ShapeDtypeStruct((1, n_e), F32)],
        scratch_shapes=[pltpu.VMEM((1, n_e), F32)],
        compiler_params=_params("arbitrary", "arbitrary"),
    )(x, mods, g, w_router, router_bias)


DISPATCH_TOKENS = 256


def _dispatch_kernel(dest_ref, h_ref, xs_in_ref, xs_ref, sem):
    del xs_in_ref
    base = pl.program_id(0) * DISPATCH_TOKENS

    def issue(r, carry):
        for k in range(TOP_K):
            slot = dest_ref[r * TOP_K + k]
            pltpu.make_async_copy(h_ref.at[pl.ds(base + r, 1)], xs_ref.at[pl.ds(slot, 1)],
                                  sem).start()
        return carry

    lax.fori_loop(0, DISPATCH_TOKENS, issue, 0)
    n = DISPATCH_TOKENS * TOP_K
    pltpu.make_async_copy(h_ref.at[pl.ds(0, n)], xs_ref.at[pl.ds(0, n)], sem).wait()


def _dispatch(h2d, dest_flat, n_slots):
    n, d = h2d.shape
    xs0 = jnp.zeros((n_slots, d), F32)
    return pl.pallas_call(
        _dispatch_kernel,
        grid=(n // DISPATCH_TOKENS,),
        in_specs=[pl.BlockSpec((DISPATCH_TOKENS * TOP_K,), lambda i: (i,),
                               memory_space=pltpu.SMEM),
                  pl.BlockSpec(memory_space=pl.ANY),
                  pl.BlockSpec(memory_space=pl.ANY)],
        out_specs=pl.BlockSpec(memory_space=pl.ANY),
        out_shape=jax.ShapeDtypeStruct((n_slots, d), F32),
        scratch_shapes=[pltpu.SemaphoreType.DMA(())],
        input_output_aliases={2: 0},
        compiler_params=_params("arbitrary"),
    )(dest_flat, h2d, xs0)


EXPERT_TILE = 256


def _expert_kernel(te_ref, na_ref, x_ref, wg_ref, wu_ref, wd_ref, y_ref, wgb_ref, wub_ref, wdb_ref):
    i = pl.program_id(0)
    active = i < na_ref[0]
    changed = jnp.logical_or(i == 0, te_ref[i] != te_ref[jnp.maximum(i - 1, 0)])

    @pl.when(jnp.logical_and(active, changed))
    def _():
        wgb_ref[...] = wg_ref[...].astype(BF16)
        wub_ref[...] = wu_ref[...].astype(BF16)
        wdb_ref[...] = wd_ref[...].astype(BF16)

    @pl.when(active)
    def _():
        xb = x_ref[...].astype(BF16)
        hid = _silu(jnp.dot(xb, wgb_ref[...], preferred_element_type=F32)) * jnp.dot(
            xb, wub_ref[...], preferred_element_type=F32)
        y_ref[...] = jnp.dot(hid.astype(BF16), wdb_ref[...], preferred_element_type=F32)

    @pl.when(jnp.logical_not(active))
    def _():
        y_ref[...] = jnp.zeros_like(y_ref)


def _experts(xs, tile_expert, n_active, w_gate, w_up, w_down, l):
    n_slots, d = xs.shape
    f = w_gate.shape[-1]
    tm = EXPERT_TILE
    grid_spec = pltpu.PrefetchScalarGridSpec(
        num_scalar_prefetch=2,
        grid=(n_slots // tm,),
        in_specs=[pl.BlockSpec((tm, d), lambda i, te, na: (i, 0)),
                  pl.BlockSpec((None, None, d, f), lambda i, te, na: (l, te[i], 0, 0)),
                  pl.BlockSpec((None, None, d, f), lambda i, te, na: (l, te[i], 0, 0)),
                  pl.BlockSpec((None, None, f, d), lambda i, te, na: (l, te[i], 0, 0))],
        out_specs=pl.BlockSpec((tm, d), lambda i, te, na: (i, 0)),
        scratch_shapes=[pltpu.VMEM((d, f), BF16), pltpu.VMEM((d, f), BF16),
                        pltpu.VMEM((f, d), BF16)],
    )
    return pl.pallas_call(
        _expert_kernel,
        grid_spec=grid_spec,
        out_shape=jax.ShapeDtypeStruct((n_slots, d), F32),
        compiler_params=_params("arbitrary"),
    )(tile_expert, n_active, xs, w_gate, w_up, w_down)


COMBINE_TOKENS = 128


def _combine_kernel(dest_ref, ys_ref, x_ref, h_ref, wt_ref, mod_ref, sg_ref, su_ref, sd_ref,
                    o_ref, buf_ref, sem):
    tm = COMBINE_TOKENS

    def issue(r, carry):
        for k in range(TOP_K):
            slot = dest_ref[r * TOP_K + k]
            pltpu.make_async_copy(ys_ref.at[pl.ds(slot, 1)], buf_ref.at[pl.ds(k * tm + r, 1)],
                                  sem).start()
        return carry

    lax.fori_loop(0, tm, issue, 0)
    hb = h_ref[...].astype(BF16)
    hid = _silu(jnp.dot(hb, sg_ref[...], preferred_element_type=F32)) * jnp.dot(
        hb, su_ref[...], preferred_element_type=F32)
    y = jnp.dot(hid.astype(BF16), sd_ref[...], preferred_element_type=F32)
    pltpu.make_async_copy(ys_ref.at[pl.ds(0, TOP_K * tm)], buf_ref, sem).wait()
    wt = wt_ref[...]
    for k in range(TOP_K):
        y = y + wt[:, k:k + 1] * buf_ref[pl.ds(k * tm, tm), :]
    o_ref[...] = x_ref[...] + mod_ref[5:6, :] * y


def _combine(ys, dest_flat, x, h, wts, mods, ws_gate_bf, ws_up_bf, ws_down_bf, l, n_ctx):
    b, t, d = x.shape
    tm = COMBINE_TOKENS
    nt = t // tm
    nct = n_ctx // tm
    fs = ws_gate_bf.shape[-1]
    row = lambda bi, i: (bi, i, 0)
    return pl.pallas_call(
        _combine_kernel,
        grid=(b, nt),
        in_specs=[pl.BlockSpec((tm * TOP_K,), lambda bi, i: (bi * nt + i,),
                               memory_space=pltpu.SMEM),
                  pl.BlockSpec(memory_space=pl.ANY),
                  pl.BlockSpec((None, tm, d), row),
                  pl.BlockSpec((None, tm, d), row),
                  pl.BlockSpec((None, tm, TOP_K), row),
                  pl.BlockSpec((None, None, N_MOD, d),
                               lambda bi, i: (l, jnp.where(i < nct, b, bi), 0, 0)),
                  pl.BlockSpec((None, d, fs), lambda bi, i: (l, 0, 0)),
                  pl.BlockSpec((None, d, fs), lambda bi, i: (l, 0, 0)),
                  pl.BlockSpec((None, fs, d), lambda bi, i: (l, 0, 0))],
        out_specs=pl.BlockSpec((None, tm, d), row),
        out_shape=jax.ShapeDtypeStruct((b, t, d), F32),
        scratch_shapes=[pltpu.VMEM((TOP_K * tm, d), F32), pltpu.SemaphoreType.DMA(())],
        compiler_params=_params("arbitrary", "arbitrary"),
    )(dest_flat, ys, x, h, wts, mods, ws_gate_bf, ws_up_bf, ws_down_bf)


def _final_kernel(x_ref, g_ref, o_ref):
    x = x_ref[...]
    ms = jnp.mean(x * x, axis=-1, keepdims=True)
    o_ref[...] = x * lax.rsqrt(ms + NORM_EPS) * g_ref[...]


def _final_norm(x, g_final, n_ctx):
    b, t, d = x.shape
    tm = ROW_TILE
    nct = n_ctx // tm
    return pl.pallas_call(
        _final_kernel,
        grid=(b, (t - n_ctx) // tm),
        in_specs=[pl.BlockSpec((None, tm, d), lambda bi, i: (bi, i + nct, 0)),
                  pl.BlockSpec((1, d), lambda bi, i: (0, 0))],
        out_specs=pl.BlockSpec((None, tm, d), lambda bi, i: (bi, i, 0)),
        out_shape=jax.ShapeDtypeStruct((b, t - n_ctx, d), F32),
        compiler_params=_params("arbitrary", "arbitrary"),
    )(x, g_final.reshape(1, d))


def _latent_permute(xs, n_ctx, to_col_major):
    b, t, d = xs.shape
    rows = (t - n_ctx) // GRID_W
    lat = xs[:, n_ctx:]
    if to_col_major:
        lat = lat.reshape(b, rows, GRID_W, d).transpose(0, 2, 1, 3)
    else:
        lat = lat.reshape(b, GRID_W, rows, d).transpose(0, 2, 1, 3)
    return jnp.concatenate([xs[:, :n_ctx], lat.reshape(b, t - n_ctx, d)], axis=1)


def _routing_plan(idx, rank, counts, n_tiles):
    n_e = counts.shape[-1]
    cnt = counts.reshape(n_e).astype(jnp.int32)
    tiles = (cnt + EXPERT_TILE - 1) // EXPERT_TILE
    tile_end = jnp.cumsum(tiles)
    start = (tile_end - tiles) * EXPERT_TILE
    onehot = idx[..., None] == jnp.arange(n_e, dtype=jnp.int32)
    dest = rank + jnp.sum(jnp.where(onehot, start, 0), axis=-1)
    tile_ids = jnp.arange(n_tiles, dtype=jnp.int32)
    tile_expert = jnp.sum(tile_ids[:, None] >= tile_end[None, :], axis=-1).astype(jnp.int32)
    tile_expert = jnp.minimum(tile_expert, n_e - 1)
    return dest.reshape(-1).astype(jnp.int32), tile_expert, tile_end[-1:].astype(jnp.int32)


def kernel(x, c, ctx, c_ctx, w_mod, b_mod, g_mix, g_ffn, w_in, conv_w, lru_w_a, lru_b_a, lru_w_i,
           lru_b_i, lru_lam, gdn_a_log, gdn_dt_bias, gdn_norm_g, w_out, w_router, router_bias,
           w_gate, w_up, w_down, ws_gate, ws_up, ws_down, g_final):
    b, seq, d = x.shape
    n_ctx = ctx.shape[1]
    t = n_ctx + seq
    depth = w_mod.shape[0]
    n_e = w_router.shape[-1]
    assert n_ctx % ROW_TILE == 0 and seq % ROW_TILE == 0 and b < 8
    n_tiles = (b * t * TOP_K) // EXPERT_TILE + n_e

    mods = _modulation(c, c_ctx, w_mod, b_mod)
    g_mix3 = g_mix.reshape(depth, 1, d)
    g_ffn3 = g_ffn.reshape(depth, 1, d)
    gn3 = gdn_norm_g.reshape(depth, 1, GDN_DV)
    rb3 = router_bias.reshape(depth, 1, n_e)
    w_out_bf = w_out.astype(BF16)
    ws_gate_bf, ws_up_bf, ws_down_bf = (w.astype(BF16) for w in (ws_gate, ws_up, ws_down))

    xs = jnp.concatenate([ctx, x], axis=1)
    for l in range(depth):
        col_major = l % 2 == 1
        if col_major:
            xs = _latent_permute(xs, n_ctx, True)
        h, pg = _inprep(xs, mods, g_mix3, w_in[l, :, MAIN_WIDTH:].astype(BF16), l, n_ctx)
        p = _inproj(h.reshape(b * t, d), w_in, l).reshape(b, t, MAIN_WIDTH)
        u = _conv(p, conv_w, l, n_ctx)
        y_lru = _lru(u, p, lru_w_a, lru_b_a, lru_w_i, lru_b_i, lru_lam, l, n_ctx)
        o_f = _gdn(u, pg, gdn_a_log, gdn_dt_bias, l, n_ctx, False)
        o_b = _gdn(u, pg, gdn_a_log, gdn_dt_bias, l, n_ctx, True)
        xs = _outproj(y_lru, o_f, o_b, p, xs, mods, gn3, w_out_bf, l, n_ctx)
        if col_major:
            xs = _latent_permute(xs, n_ctx, False)

        h2, idx, wts, rank, counts = _router(xs, mods, g_ffn3, w_router, rb3, l, n_ctx)
        dest, tile_expert, n_active = _routing_plan(idx, rank, counts, n_tiles)
        xsorted = _dispatch(h2.reshape(b * t, d), dest, n_tiles * EXPERT_TILE)
        ysorted = _experts(xsorted, tile_expert, n_active, w_gate, w_up, w_down, l)
        xs = _combine(ysorted, dest, xs, h2, wts, mods, ws_gate_bf, ws_up_bf, ws_down_bf, l, n_ctx)
    return _final_norm(xs, g_final, n_ctx)
```

```python
import functools

import jax
import jax.numpy as jnp
from jax import lax
from jax.experimental import pallas as pl
from jax.experimental.pallas import tpu as pltpu

F32 = jnp.float32
BF16 = jnp.bfloat16
HI = lax.Precision.HIGHEST

GRID_W = 64
N_MOD = 6
NORM_EPS = 1e-6
LRU_WIDTH = 1024
LRU_BLOCK = 128
LRU_C = 8.0
GDN_HEADS = 8
GDN_DK = 128
GDN_DV = 128
QK_WIDTH = GDN_HEADS * GDN_DK
V_WIDTH = GDN_HEADS * GDN_DV
CHUNK = 64
CONV_W = 4
CONV_PAD_LEFT = 2
CONV_CH = LRU_WIDTH + 2 * QK_WIDTH + V_WIDTH
MAIN_WIDTH = CONV_CH + LRU_WIDTH + V_WIDTH
N_GATE = 4 * GDN_HEADS
TOP_K = 8
ROUTED_SCALE = 2.5

LANE = 128
ROW_TILE = 256
VMEM_LIMIT = 56 * 1024 * 1024

NT_DIMS = (((1,), (1,)), ((), ()))
TN_DIMS = (((0,), (0,)), ((), ()))


def _params(*sem):
    return pltpu.CompilerParams(dimension_semantics=sem, vmem_limit_bytes=VMEM_LIMIT)


def _silu(x):
    return x * jax.nn.sigmoid(x)


def _norm_mod(x, g, shift, scale):
    ms = jnp.mean(x * x, axis=-1, keepdims=True)
    return (x * lax.rsqrt(ms + NORM_EPS) * g) * (1.0 + scale) + shift


def _mod_kernel(s_ref, w_ref, b_ref, o_ref):
    s = _silu(s_ref[...])
    o_ref[...] = jnp.dot(s.astype(BF16), w_ref[...].astype(BF16),
                         preferred_element_type=F32) + b_ref[...]


def _modulation(c, c_ctx, w_mod, b_mod):
    depth, d, nd = w_mod.shape
    b = c.shape[0]
    rows = jnp.concatenate([c, c_ctx[None], jnp.zeros((8 - b - 1, d), F32)], axis=0)
    tn = 1024
    out = pl.pallas_call(
        _mod_kernel,
        name="modulation",
        grid=(depth, nd // tn),
        in_specs=[pl.BlockSpec((8, d), lambda l, j: (0, 0)),
                  pl.BlockSpec((None, d, tn), lambda l, j: (l, 0, j)),
                  pl.BlockSpec((None, 1, tn), lambda l, j: (l, 0, j))],
        out_specs=pl.BlockSpec((None, 8, tn), lambda l, j: (l, 0, j)),
        out_shape=jax.ShapeDtypeStruct((depth, 8, nd), F32),
        compiler_params=_params("arbitrary", "arbitrary"),
    )(rows, w_mod, b_mod.reshape(depth, 1, nd))
    return out.reshape(depth, 8, N_MOD, d)


def _inprep_kernel(x_ref, mod_ref, g_ref, wg_ref, h_ref, pg_ref):
    h = _norm_mod(x_ref[...], g_ref[...], mod_ref[0:1, :], mod_ref[1:2, :])
    hb = h.astype(BF16)
    h_ref[...] = hb
    pg_ref[...] = jnp.dot(hb, wg_ref[...], preferred_element_type=F32)


def _inprep(x, mods, g, w_gatecols, l, n_ctx):
    b, t, d = x.shape
    tm = ROW_TILE
    nct = n_ctx // tm
    return pl.pallas_call(
        _inprep_kernel,
        name="inprep",
        grid=(b, t // tm),
        in_specs=[pl.BlockSpec((None, tm, d), lambda bi, i: (bi, i, 0)),
                  pl.BlockSpec((None, None, N_MOD, d),
                               lambda bi, i: (l, jnp.where(i < nct, b, bi), 0, 0)),
                  pl.BlockSpec((None, 1, d), lambda bi, i: (l, 0, 0)),
                  pl.BlockSpec((d, N_GATE), lambda bi, i: (0, 0))],
        out_specs=[pl.BlockSpec((None, tm, d), lambda bi, i: (bi, i, 0)),
                   pl.BlockSpec((None, tm, N_GATE), lambda bi, i: (bi, i, 0))],
        out_shape=[jax.ShapeDtypeStruct((b, t, d), BF16),
                   jax.ShapeDtypeStruct((b, t, N_GATE), F32)],
        compiler_params=_params("arbitrary", "arbitrary"),
    )(x, mods, g, w_gatecols)


def _matmul_kernel(h_ref, w_ref, o_ref, wb_ref):
    @pl.when(pl.program_id(1) == 0)
    def _():
        wb_ref[...] = w_ref[...].astype(BF16)

    o_ref[...] = jnp.dot(h_ref[...], wb_ref[...], preferred_element_type=F32)


def _inproj(h2d, w_in, l):
    n, d = h2d.shape
    tn, tm = 1024, 512
    return pl.pallas_call(
        _matmul_kernel,
        name="inproj",
        grid=(MAIN_WIDTH // tn, n // tm),
        in_specs=[pl.BlockSpec((tm, d), lambda j, i: (i, 0)),
                  pl.BlockSpec((None, d, tn), lambda j, i: (l, 0, j))],
        out_specs=pl.BlockSpec((tm, tn), lambda j, i: (i, j)),
        out_shape=jax.ShapeDtypeStruct((n, MAIN_WIDTH), F32),
        scratch_shapes=[pltpu.VMEM((d, tn), BF16)],
        compiler_params=_params("arbitrary", "arbitrary"),
    )(h2d, w_in)


def _conv_kernel(p_ref, cw_ref, u_ref, *, n_ctx):
    kind = pl.program_id(1) // GDN_HEADS
    x = p_ref[...]
    t_len = x.shape[0]
    t = lax.broadcasted_iota(jnp.int32, x.shape, 0)
    seg_t = jnp.where(t >= n_ctx, 1, 0)
    acc = jnp.zeros_like(x)
    for j in range(CONV_W):
        off = j - CONV_PAD_LEFT
        if off == 0:
            xs = x
        else:
            xs = pltpu.roll(x, (-off) % t_len, 0)
            s = t + off
            seg_s = jnp.where(s >= n_ctx, 1, 0) + jnp.where(s >= t_len, 1, 0) - jnp.where(s < 0, 1, 0)
            xs = jnp.where(seg_s == seg_t, xs, 0.0)
        acc = acc + xs * cw_ref[j:j + 1, :]

    @pl.when(kind == 0)
    def _():
        u_ref[...] = acc

    @pl.when(kind == 3)
    def _():
        u_ref[...] = _silu(acc)

    @pl.when(jnp.logical_or(kind == 1, kind == 2))
    def _():
        y = _silu(acc)
        nrm = y * lax.rsqrt(jnp.sum(y * y, axis=-1, keepdims=True) + NORM_EPS)
        u_ref[...] = nrm * jnp.where(kind == 1, GDN_DK ** -0.5, 1.0)


def _conv(p, conv_w, l, n_ctx):
    b, t, _ = p.shape
    return pl.pallas_call(
        functools.partial(_conv_kernel, n_ctx=n_ctx),
        name="conv",
        grid=(b, CONV_CH // LANE),
        in_specs=[pl.BlockSpec((None, t, LANE), lambda bi, j: (bi, 0, j)),
                  pl.BlockSpec((None, CONV_W, LANE), lambda bi, j: (l, 0, j))],
        out_specs=pl.BlockSpec((None, t, LANE), lambda bi, j: (bi, 0, j)),
        out_shape=jax.ShapeDtypeStruct((b, t, CONV_CH), F32),
        compiler_params=_params("arbitrary", "arbitrary"),
    )(p, conv_w)


LRU_LANES = 256


def _lru_kernel(u_ref, y_ref, wa_ref, ba_ref, wi_ref, bi_ref, lam_ref, o_ref,
                af_ref, bf_ref, ab_ref, bb_ref, *, n_ctx):
    t_len, width = u_ref.shape
    nblk = width // LRU_BLOCK
    tm = ROW_TILE

    def gates(i, carry):
        r0 = pl.multiple_of(i * tm, tm)
        u = u_ref[pl.ds(r0, tm), :]
        ub = u.astype(BF16)
        for d, (a_ref, b_ref) in enumerate(((af_ref, bf_ref), (ab_ref, bb_ref))):
            pre_a = jnp.concatenate(
                [jnp.dot(ub[:, n * LRU_BLOCK:(n + 1) * LRU_BLOCK], wa_ref[d, n].astype(BF16),
                         preferred_element_type=F32) for n in range(nblk)], axis=-1)
            pre_i = jnp.concatenate(
                [jnp.dot(ub[:, n * LRU_BLOCK:(n + 1) * LRU_BLOCK], wi_ref[d, n].astype(BF16),
                         preferred_element_type=F32) for n in range(nblk)], axis=-1)
            r = jax.nn.sigmoid(pre_a + ba_ref[d:d + 1, :])
            gi = jax.nn.sigmoid(pre_i + bi_ref[d:d + 1, :])
            lam = lam_ref[d:d + 1, :]
            log_sig = jnp.minimum(lam, 0.0) - jnp.log(1.0 + jnp.exp(-jnp.abs(lam)))
            log_a = LRU_C * r * log_sig
            a = jnp.exp(log_a)
            bcoef = jnp.sqrt(jnp.maximum(1.0 - jnp.exp(2.0 * log_a), 0.0)) * (gi * u)
            a_ref[pl.ds(r0, tm), :] = a
            b_ref[pl.ds(r0, tm), :] = bcoef
        return carry

    lax.fori_loop(0, t_len // tm, gates, 0)

    def scan(gi, carry):
        hf, hb = carry
        base = pl.multiple_of(gi * 8, 8)
        bbase = jnp.where(base < n_ctx, n_ctx - 1 - base, t_len + n_ctx - 1 - base)
        for j in range(8):
            tf = base + j
            tb = bbase - j
            hf = af_ref[pl.ds(tf, 1), :] * hf + bf_ref[pl.ds(tf, 1), :]
            af_ref[pl.ds(tf, 1), :] = hf
            hb = ab_ref[pl.ds(tb, 1), :] * hb + bb_ref[pl.ds(tb, 1), :]
            ab_ref[pl.ds(tb, 1), :] = hb
        return hf, hb

    zero = jnp.zeros((1, width), F32)
    lax.fori_loop(0, t_len // 8, scan, (zero, zero))
    o_ref[...] = jax.nn.gelu(y_ref[...], approximate=True) * (af_ref[...] + ab_ref[...])


def _lru(u, p, lru_w_a, lru_b_a, lru_w_i, lru_b_i, lru_lam, l, n_ctx):
    b, t, _ = u.shape
    w = LRU_LANES
    nb = w // LRU_BLOCK
    y_off = CONV_CH // w
    wspec = pl.BlockSpec((None, 2, nb, LRU_BLOCK, LRU_BLOCK), lambda bi, j: (l, 0, j, 0, 0))
    vspec = pl.BlockSpec((None, 2, w), lambda bi, j: (l, 0, j))
    return pl.pallas_call(
        functools.partial(_lru_kernel, n_ctx=n_ctx),
        name="lru",
        grid=(b, LRU_WIDTH // w),
        in_specs=[pl.BlockSpec((None, t, w), lambda bi, j: (bi, 0, j)),
                  pl.BlockSpec((None, t, w), lambda bi, j: (bi, 0, y_off + j)),
                  wspec, vspec, wspec, vspec, vspec],
        out_specs=pl.BlockSpec((None, t, w), lambda bi, j: (bi, 0, j)),
        out_shape=jax.ShapeDtypeStruct((b, t, LRU_WIDTH), F32),
        scratch_shapes=[pltpu.VMEM((t, w), F32)] * 4,
        compiler_params=_params("arbitrary", "arbitrary"),
    )(u, p, lru_w_a, lru_b_a, lru_w_i, lru_b_i, lru_lam)


def _split(a):
    hi = a.astype(BF16).astype(F32)
    return hi, a - hi


def _lhs3(a):
    hi, lo = _split(a)
    return jnp.concatenate([hi, lo, hi], axis=1).astype(BF16)


def _rhs3(b):
    hi, lo = _split(b)
    return jnp.concatenate([hi, hi, lo], axis=0).astype(BF16)


def _gdn_kernel(q_ref, k_ref, v_ref, gt_ref, alog_ref, dtb_ref, o_ref, s_ref, *, reverse):
    c, h_n, dk, dv = CHUNK, GDN_HEADS, GDN_DK, GDN_DV
    d = int(reverse)

    @pl.when(pl.program_id(1) == 0)
    def _():
        s_ref[...] = jnp.zeros_like(s_ref)

    gt = gt_ref[...]
    beta = jax.nn.sigmoid(gt[:, d * h_n:(d + 1) * h_n])
    xa = gt[:, (2 + d) * h_n:(3 + d) * h_n] + dtb_ref[d:d + 1, :]
    softplus = jnp.maximum(xa, 0.0) + jnp.log(1.0 + jnp.exp(-jnp.abs(xa)))
    g = -jnp.exp(alog_ref[d:d + 1, :]) * softplus

    ri = lax.broadcasted_iota(jnp.int32, (c, c), 0)
    ci = lax.broadcasted_iota(jnp.int32, (c, c), 1)
    incl = (ri <= ci) if reverse else (ri >= ci)
    strict = (ri < ci) if reverse else (ri > ci)
    ones_incl = jnp.where(incl, 1.0, 0.0)
    cum = jnp.dot(ones_incl, g, precision=HI, preferred_element_type=F32)
    cum_t = lax.dot_general(g, ones_incl, (((0,), (1,)), ((), ())), precision=HI,
                            preferred_element_type=F32)
    total = cum[0:1, :] if reverse else cum[c - 1:c, :]
    e_cum = jnp.exp(cum)
    e_rest = jnp.exp(total - cum)
    e_total = jnp.exp(total)

    gammas, negs, xs = [], [], []
    for h in range(h_n):
        sl = slice(h * dk, (h + 1) * dk)
        k = k_ref[:, sl]
        kb = k.astype(BF16)
        diff = jnp.where(incl, cum[:, h:h + 1] - cum_t[h:h + 1, :], 0.0)
        gamma = jnp.where(incl, jnp.exp(diff), 0.0)
        bcol = beta[:, h:h + 1]
        kk = lax.dot_general(kb, kb, NT_DIMS, preferred_element_type=F32)
        gammas.append(gamma)
        negs.append(jnp.where(strict, -(bcol * kk * gamma), 0.0))
        xs.append(jnp.concatenate([bcol * v_ref[:, sl], bcol * k * e_cum[:, h:h + 1]], axis=-1))

    for level in range(6):
        lhs = [_lhs3(p) for p in negs]
        xs = [x + jnp.dot(a, _rhs3(x), preferred_element_type=F32) for a, x in zip(lhs, xs)]
        if level < 5:
            negs = [jnp.dot(a, _rhs3(p), preferred_element_type=F32) for a, p in zip(lhs, negs)]

    for h in range(h_n):
        sl = slice(h * dk, (h + 1) * dk)
        q = q_ref[:, sl]
        k = k_ref[:, sl]
        u = xs[h][:, :dv]
        w = xs[h][:, dv:]
        qk = lax.dot_general(q.astype(BF16), k.astype(BF16), NT_DIMS,
                             preferred_element_type=F32) * gammas[h]
        q_dec = q * e_cum[:, h:h + 1]
        k_dec = k * e_rest[:, h:h + 1]
        s = s_ref[h]
        ws = jnp.dot(jnp.concatenate([w, q_dec], axis=0).astype(BF16), s.astype(BF16),
                     preferred_element_type=F32)
        v_new = u - ws[:c]
        vb = v_new.astype(BF16)
        o_ref[:, h * dv:(h + 1) * dv] = ws[c:] + jnp.dot(qk.astype(BF16), vb,
                                                         preferred_element_type=F32)
        s_ref[h] = e_total[:, h:h + 1] * s + lax.dot_general(
            k_dec.astype(BF16), vb, TN_DIMS, preferred_element_type=F32)


def _gdn(u, pg, gdn_a_log, gdn_dt_bias, l, n_ctx, reverse):
    b, t, _ = u.shape
    nc = t // CHUNK
    ncc = n_ctx // CHUNK

    def chunk(s):
        if not reverse:
            return s
        return jnp.where(s < ncc, ncc - 1 - s, nc + ncc - 1 - s)

    def qkv_spec(col):
        return pl.BlockSpec((None, CHUNK, QK_WIDTH), lambda bi, s: (bi, chunk(s), col))

    small = pl.BlockSpec((None, 2, GDN_HEADS), lambda bi, s: (l, 0, 0))
    return pl.pallas_call(
        functools.partial(_gdn_kernel, reverse=reverse),
        name="gdn_bwd" if reverse else "gdn_fwd",
        grid=(b, nc),
        in_specs=[qkv_spec(1), qkv_spec(2), qkv_spec(3),
                  pl.BlockSpec((None, CHUNK, N_GATE), lambda bi, s: (bi, chunk(s), 0)),
                  small, small],
        out_specs=pl.BlockSpec((None, CHUNK, V_WIDTH), lambda bi, s: (bi, chunk(s), 0)),
        out_shape=jax.ShapeDtypeStruct((b, t, V_WIDTH), F32),
        scratch_shapes=[pltpu.VMEM((GDN_HEADS, GDN_DK, GDN_DV), F32)],
        compiler_params=_params("arbitrary", "arbitrary"),
    )(u, u, u, pg, gdn_a_log, gdn_dt_bias)


def _outproj_kernel(ylru_ref, of_ref, ob_ref, z_ref, x_ref, mod_ref, gn_ref, w_ref, o_ref):
    o = of_ref[...] + ob_ref[...]
    z = z_ref[...]
    parts = [ylru_ref[...]]
    for h in range(GDN_HEADS):
        sl = slice(h * GDN_DV, (h + 1) * GDN_DV)
        oh = o[:, sl]
        ms = jnp.mean(oh * oh, axis=-1, keepdims=True)
        parts.append(oh * lax.rsqrt(ms + NORM_EPS) * gn_ref[...] * _silu(z[:, sl]))
    feats = jnp.concatenate(parts, axis=-1).astype(BF16)
    y = jnp.dot(feats, w_ref[...], preferred_element_type=F32)
    o_ref[...] = x_ref[...] + mod_ref[2:3, :] * y


def _outproj(y_lru, o_f, o_b, p, x, mods, gdn_norm_g, w_out_bf, l, n_ctx):
    b, t, d = x.shape
    tm = ROW_TILE
    nct = n_ctx // tm
    z_off = (CONV_CH + LRU_WIDTH) // V_WIDTH
    row = lambda bi, i: (bi, i, 0)
    return pl.pallas_call(
        _outproj_kernel,
        name="outproj",
        grid=(b, t // tm),
        in_specs=[pl.BlockSpec((None, tm, LRU_WIDTH), row),
                  pl.BlockSpec((None, tm, V_WIDTH), row),
                  pl.BlockSpec((None, tm, V_WIDTH), row),
                  pl.BlockSpec((None, tm, V_WIDTH), lambda bi, i: (bi, i, z_off)),
                  pl.BlockSpec((None, tm, d), row),
                  pl.BlockSpec((None, None, N_MOD, d),
                               lambda bi, i: (l, jnp.where(i < nct, b, bi), 0, 0)),
                  pl.BlockSpec((None, 1, GDN_DV), lambda bi, i: (l, 0, 0)),
                  pl.BlockSpec((None, LRU_WIDTH + V_WIDTH, d), lambda bi, i: (l, 0, 0))],
        out_specs=pl.BlockSpec((None, tm, d), row),
        out_shape=jax.ShapeDtypeStruct((b, t, d), F32),
        compiler_params=_params("arbitrary", "arbitrary"),
    )(y_lru, o_f, o_b, p, x, mods, gdn_norm_g, w_out_bf)


def _router_kernel(x_ref, mod_ref, g_ref, wr_ref, rb_ref, h_ref, idx_ref, wt_ref, rank_ref,
                   cnt_ref, run_ref):
    first = jnp.logical_and(pl.program_id(0) == 0, pl.program_id(1) == 0)

    @pl.when(first)
    def _():
        run_ref[...] = jnp.zeros_like(run_ref)

    h = _norm_mod(x_ref[...], g_ref[...], mod_ref[3:4, :], mod_ref[4:5, :])
    scores = jax.nn.sigmoid(jnp.dot(h, wr_ref[...], precision=HI, preferred_element_type=F32))
    tm, n_e = scores.shape
    nb = h.shape[1] // LANE
    for s in range(nb):
        h_ref[pl.ds(s, tm, stride=nb), :] = h[:, s * LANE:(s + 1) * LANE]
    lane = lax.broadcasted_iota(jnp.int32, (tm, n_e), 1)
    kl = lax.broadcasted_iota(jnp.int32, (tm, TOP_K), 1)
    biased = scores + rb_ref[...]
    idx = jnp.zeros((tm, TOP_K), jnp.int32)
    sel = jnp.zeros((tm, TOP_K), F32)
    maskf = jnp.zeros((tm, n_e), F32)
    onehots = []
    for k in range(TOP_K):
        m = jnp.max(biased, axis=-1, keepdims=True)
        ik = jnp.min(jnp.where(biased == m, lane, n_e), axis=-1, keepdims=True)
        oh = lane == ik
        onehots.append(oh)
        sk = jnp.sum(jnp.where(oh, scores, 0.0), axis=-1, keepdims=True)
        idx = jnp.where(kl == k, ik, idx)
        sel = jnp.where(kl == k, sk, sel)
        maskf = jnp.where(oh, 1.0, maskf)
        biased = jnp.where(oh, -jnp.inf, biased)
    wt_ref[...] = sel / jnp.sum(sel, axis=-1, keepdims=True) * ROUTED_SCALE
    idx_ref[...] = idx

    ri = lax.broadcasted_iota(jnp.int32, (tm, tm), 0)
    ci = lax.broadcasted_iota(jnp.int32, (tm, tm), 1)
    before = jnp.where(ri > ci, 1.0, 0.0).astype(BF16)
    prior = jnp.dot(before, maskf.astype(BF16), preferred_element_type=F32) + run_ref[...]
    rank = jnp.zeros((tm, TOP_K), F32)
    for k in range(TOP_K):
        rk = jnp.sum(jnp.where(onehots[k], prior, 0.0), axis=-1, keepdims=True)
        rank = jnp.where(kl == k, rk, rank)
    rank_ref[...] = rank.astype(jnp.int32)
    run_ref[...] = run_ref[...] + jnp.sum(maskf, axis=0, keepdims=True)
    cnt_ref[...] = run_ref[...]


def _router(x, mods, g, w_router, router_bias, l, n_ctx):
    b, t, d = x.shape
    n_e = w_router.shape[-1]
    tm = ROW_TILE
    nt = t // tm
    nct = n_ctx // tm
    nb = d // LANE
    row = lambda bi, i: (bi, i, 0)
    return pl.pallas_call(
        _router_kernel,
        name="router",
        grid=(b, nt),
        in_specs=[pl.BlockSpec((None, tm, d), row),
                  pl.BlockSpec((None, None, N_MOD, d),
                               lambda bi, i: (l, jnp.where(i < nct, b, bi), 0, 0)),
                  pl.BlockSpec((None, 1, d), lambda bi, i: (l, 0, 0)),
                  pl.BlockSpec((None, d, n_e), lambda bi, i: (l, 0, 0)),
                  pl.BlockSpec((None, 1, n_e), lambda bi, i: (l, 0, 0))],
        out_specs=[pl.BlockSpec((tm * nb, LANE), lambda bi, i: (bi * nt + i, 0)),
                   pl.BlockSpec((None, tm, TOP_K), row),
                   pl.BlockSpec((None, tm, TOP_K), row),
                   pl.BlockSpec((None, tm, TOP_K), row),
                   pl.BlockSpec((1, n_e), lambda bi, i: (0, 0))],
        out_shape=[jax.ShapeDtypeStruct((b * t * nb, LANE), F32),
                   jax.ShapeDtypeStruct((b, t, TOP_K), jnp.int32),
                   jax.ShapeDtypeStruct((b, t, TOP_K), F32),
                   jax.ShapeDtypeStruct((b, t, TOP_K), jnp.int32),
                   jax.ShapeDtypeStruct((1, n_e), F32)],
        scratch_shapes=[pltpu.VMEM((1, n_e), F32)],
        compiler_params=_params("arbitrary", "arbitrary"),
    )(x, mods, g, w_router, router_bias)


DISPATCH_TOKENS = 256


def _dispatch_kernel(dest_ref, h_ref, xs_in_ref, xs_ref, sem, *, nb):
    del xs_in_ref

    def issue(r, carry):
        src = h_ref.at[pl.ds(pl.multiple_of(r * nb, nb), nb)]
        for k in range(TOP_K):
            slot = dest_ref[r * TOP_K + k]
            pltpu.make_async_copy(src, xs_ref.at[pl.ds(pl.multiple_of(slot * nb, nb), nb)],
                                  sem).start()
        return carry

    lax.fori_loop(0, DISPATCH_TOKENS, issue, 0)
    for k in range(TOP_K):
        pltpu.make_async_copy(h_ref, xs_ref.at[pl.ds(0, DISPATCH_TOKENS * nb)], sem).wait()


def _dispatch(h_blocks, dest_flat, n_slots, nb):
    n = h_blocks.shape[0] // nb
    xs0 = jnp.zeros((n_slots * nb, LANE), F32)
    return pl.pallas_call(
        functools.partial(_dispatch_kernel, nb=nb),
        name="dispatch",
        grid=(n // DISPATCH_TOKENS,),
        in_specs=[pl.BlockSpec((DISPATCH_TOKENS * TOP_K,), lambda i: (i,),
                               memory_space=pltpu.SMEM),
                  pl.BlockSpec((DISPATCH_TOKENS * nb, LANE), lambda i: (i, 0)),
                  pl.BlockSpec(memory_space=pl.ANY)],
        out_specs=pl.BlockSpec(memory_space=pl.ANY),
        out_shape=jax.ShapeDtypeStruct((n_slots * nb, LANE), F32),
        scratch_shapes=[pltpu.SemaphoreType.DMA(())],
        input_output_aliases={2: 0},
        compiler_params=_params("arbitrary"),
    )(dest_flat, h_blocks, xs0)


EXPERT_TILE = 256


def _expert_kernel(te_ref, na_ref, x_ref, wg_ref, wu_ref, wd_ref, y_ref, wgb_ref, wub_ref, wdb_ref):
    i = pl.program_id(0)
    active = i < na_ref[0]
    changed = jnp.logical_or(i == 0, te_ref[i] != te_ref[jnp.maximum(i - 1, 0)])

    @pl.when(jnp.logical_and(active, changed))
    def _():
        wgb_ref[...] = wg_ref[...].astype(BF16)
        wub_ref[...] = wu_ref[...].astype(BF16)
        wdb_ref[...] = wd_ref[...].astype(BF16)

    @pl.when(active)
    def _():
        tm = y_ref.shape[0]
        nb = x_ref.shape[0] // tm
        xb = jnp.concatenate([x_ref[pl.ds(s, tm, stride=nb), :] for s in range(nb)],
                             axis=-1).astype(BF16)
        hid = _silu(jnp.dot(xb, wgb_ref[...], preferred_element_type=F32)) * jnp.dot(
            xb, wub_ref[...], preferred_element_type=F32)
        y_ref[...] = jnp.dot(hid.astype(BF16), wdb_ref[...], preferred_element_type=F32)

    @pl.when(jnp.logical_not(active))
    def _():
        y_ref[...] = jnp.zeros_like(y_ref)


def _experts(xs, tile_expert, n_active, w_gate, w_up, w_down, l):
    d, f = w_gate.shape[-2:]
    nb = d // LANE
    n_slots = xs.shape[0] // nb
    tm = EXPERT_TILE
    grid_spec = pltpu.PrefetchScalarGridSpec(
        num_scalar_prefetch=2,
        grid=(n_slots // tm,),
        in_specs=[pl.BlockSpec((tm * nb, LANE), lambda i, te, na: (i, 0)),
                  pl.BlockSpec((None, None, d, f), lambda i, te, na: (l, te[i], 0, 0)),
                  pl.BlockSpec((None, None, d, f), lambda i, te, na: (l, te[i], 0, 0)),
                  pl.BlockSpec((None, None, f, d), lambda i, te, na: (l, te[i], 0, 0))],
        out_specs=pl.BlockSpec((tm, d), lambda i, te, na: (i, 0)),
        scratch_shapes=[pltpu.VMEM((d, f), BF16), pltpu.VMEM((d, f), BF16),
                        pltpu.VMEM((f, d), BF16)],
    )
    return pl.pallas_call(
        _expert_kernel,
        name="experts",
        grid_spec=grid_spec,
        out_shape=jax.ShapeDtypeStruct((n_slots, d), F32),
        compiler_params=_params("arbitrary"),
    )(tile_expert, n_active, xs, w_gate, w_up, w_down)


COMBINE_TOKENS = 128


def _combine_kernel(dest_ref, ys_ref, x_ref, g_ref, wt_ref, mod_ref, sg_ref, su_ref, sd_ref,
                    o_ref, buf_ref, sem):
    tm = COMBINE_TOKENS

    def issue(r, carry):
        for k in range(TOP_K):
            slot = dest_ref[r * TOP_K + k]
            pltpu.make_async_copy(ys_ref.at[pl.ds(slot, 1)], buf_ref.at[pl.ds(k * tm + r, 1)],
                                  sem).start()
        return carry

    lax.fori_loop(0, tm, issue, 0)
    x = x_ref[...]
    hb = _norm_mod(x, g_ref[...], mod_ref[3:4, :], mod_ref[4:5, :]).astype(BF16)
    hid = _silu(jnp.dot(hb, sg_ref[...], preferred_element_type=F32)) * jnp.dot(
        hb, su_ref[...], preferred_element_type=F32)
    y = jnp.dot(hid.astype(BF16), sd_ref[...], preferred_element_type=F32)
    pltpu.make_async_copy(ys_ref.at[pl.ds(0, TOP_K * tm)], buf_ref, sem).wait()
    wt = wt_ref[...]
    for k in range(TOP_K):
        y = y + wt[:, k:k + 1] * buf_ref[pl.ds(k * tm, tm), :]
    o_ref[...] = x + mod_ref[5:6, :] * y


def _combine(ys, dest_flat, x, g, wts, mods, ws_gate_bf, ws_up_bf, ws_down_bf, l, n_ctx):
    b, t, d = x.shape
    tm = COMBINE_TOKENS
    nt = t // tm
    nct = n_ctx // tm
    fs = ws_gate_bf.shape[-1]
    row = lambda bi, i: (bi, i, 0)
    return pl.pallas_call(
        _combine_kernel,
        name="combine",
        grid=(b, nt),
        in_specs=[pl.BlockSpec((tm * TOP_K,), lambda bi, i: (bi * nt + i,),
                               memory_space=pltpu.SMEM),
                  pl.BlockSpec(memory_space=pl.ANY),
                  pl.BlockSpec((None, tm, d), row),
                  pl.BlockSpec((None, 1, d), lambda bi, i: (l, 0, 0)),
                  pl.BlockSpec((None, tm, TOP_K), row),
                  pl.BlockSpec((None, None, N_MOD, d),
                               lambda bi, i: (l, jnp.where(i < nct, b, bi), 0, 0)),
                  pl.BlockSpec((None, d, fs), lambda bi, i: (l, 0, 0)),
                  pl.BlockSpec((None, d, fs), lambda bi, i: (l, 0, 0)),
                  pl.BlockSpec((None, fs, d), lambda bi, i: (l, 0, 0))],
        out_specs=pl.BlockSpec((None, tm, d), row),
        out_shape=jax.ShapeDtypeStruct((b, t, d), F32),
        scratch_shapes=[pltpu.VMEM((TOP_K * tm, d), F32), pltpu.SemaphoreType.DMA(())],
        compiler_params=_params("arbitrary", "arbitrary"),
    )(dest_flat, ys, x, g, wts, mods, ws_gate_bf, ws_up_bf, ws_down_bf)


def _final_kernel(x_ref, g_ref, o_ref):
    x = x_ref[...]
    ms = jnp.mean(x * x, axis=-1, keepdims=True)
    o_ref[...] = x * lax.rsqrt(ms + NORM_EPS) * g_ref[...]


def _final_norm(x, g_final, n_ctx):
    b, t, d = x.shape
    tm = ROW_TILE
    nct = n_ctx // tm
    return pl.pallas_call(
        _final_kernel,
        name="final_norm",
        grid=(b, (t - n_ctx) // tm),
        in_specs=[pl.BlockSpec((None, tm, d), lambda bi, i: (bi, i + nct, 0)),
                  pl.BlockSpec((1, d), lambda bi, i: (0, 0))],
        out_specs=pl.BlockSpec((None, tm, d), lambda bi, i: (bi, i, 0)),
        out_shape=jax.ShapeDtypeStruct((b, t - n_ctx, d), F32),
        compiler_params=_params("arbitrary", "arbitrary"),
    )(x, g_final.reshape(1, d))


def _latent_permute(xs, n_ctx, to_col_major):
    b, t, d = xs.shape
    rows = (t - n_ctx) // GRID_W
    lat = xs[:, n_ctx:]
    if to_col_major:
        lat = lat.reshape(b, rows, GRID_W, d).transpose(0, 2, 1, 3)
    else:
        lat = lat.reshape(b, GRID_W, rows, d).transpose(0, 2, 1, 3)
    return jnp.concatenate([xs[:, :n_ctx], lat.reshape(b, t - n_ctx, d)], axis=1)


def _routing_plan(idx, rank, counts, n_tiles):
    n_e = counts.shape[-1]
    cnt = counts.reshape(n_e).astype(jnp.int32)
    tiles = (cnt + EXPERT_TILE - 1) // EXPERT_TILE
    tile_end = jnp.cumsum(tiles)
    start = (tile_end - tiles) * EXPERT_TILE
    onehot = idx[..., None] == jnp.arange(n_e, dtype=jnp.int32)
    dest = rank + jnp.sum(jnp.where(onehot, start, 0), axis=-1)
    tile_ids = jnp.arange(n_tiles, dtype=jnp.int32)
    tile_expert = jnp.sum(tile_ids[:, None] >= tile_end[None, :], axis=-1).astype(jnp.int32)
    tile_expert = jnp.minimum(tile_expert, n_e - 1)
    return dest.reshape(-1).astype(jnp.int32), tile_expert, tile_end[-1:].astype(jnp.int32)


def kernel(x, c, ctx, c_ctx, w_mod, b_mod, g_mix, g_ffn, w_in, conv_w, lru_w_a, lru_b_a, lru_w_i,
           lru_b_i, lru_lam, gdn_a_log, gdn_dt_bias, gdn_norm_g, w_out, w_router, router_bias,
           w_gate, w_up, w_down, ws_gate, ws_up, ws_down, g_final):
    b, seq, d = x.shape
    n_ctx = ctx.shape[1]
    t = n_ctx + seq
    depth = w_mod.shape[0]
    n_e = w_router.shape[-1]
    assert n_ctx % ROW_TILE == 0 and seq % ROW_TILE == 0 and b < 8
    n_tiles = (b * t * TOP_K) // EXPERT_TILE + n_e

    mods = _modulation(c, c_ctx, w_mod, b_mod)
    g_mix3 = g_mix.reshape(depth, 1, d)
    g_ffn3 = g_ffn.reshape(depth, 1, d)
    gn3 = gdn_norm_g.reshape(depth, 1, GDN_DV)
    rb3 = router_bias.reshape(depth, 1, n_e)
    w_out_bf = w_out.astype(BF16)
    ws_gate_bf, ws_up_bf, ws_down_bf = (w.astype(BF16) for w in (ws_gate, ws_up, ws_down))

    xs = jnp.concatenate([ctx, x], axis=1)
    for l in range(depth):
        col_major = l % 2 == 1
        if col_major:
            xs = _latent_permute(xs, n_ctx, True)
        h, pg = _inprep(xs, mods, g_mix3, w_in[l, :, MAIN_WIDTH:].astype(BF16), l, n_ctx)
        p = _inproj(h.reshape(b * t, d), w_in, l).reshape(b, t, MAIN_WIDTH)
        u = _conv(p, conv_w, l, n_ctx)
        y_lru = _lru(u, p, lru_w_a, lru_b_a, lru_w_i, lru_b_i, lru_lam, l, n_ctx)
        o_f = _gdn(u, pg, gdn_a_log, gdn_dt_bias, l, n_ctx, False)
        o_b = _gdn(u, pg, gdn_a_log, gdn_dt_bias, l, n_ctx, True)
        xs = _outproj(y_lru, o_f, o_b, p, xs, mods, gn3, w_out_bf, l, n_ctx)
        if col_major:
            xs = _latent_permute(xs, n_ctx, False)

        h2, idx, wts, rank, counts = _router(xs, mods, g_ffn3, w_router, rb3, l, n_ctx)
        dest, tile_expert, n_active = _routing_plan(idx, rank, counts, n_tiles)
        xsorted = _dispatch(h2, dest, n_tiles * EXPERT_TILE, d // LANE)
        ysorted = _experts(xsorted, tile_expert, n_active, w_gate, w_up, w_down, l)
        xs = _combine(ysorted, dest, xs, g_ffn3, wts, mods, ws_gate_bf, ws_up_bf, ws_down_bf, l,
                      n_ctx)
    return _final_norm(xs, g_final, n_ctx)
```

```python
import functools

import jax
import jax.numpy as jnp
from jax import lax
from jax.experimental import pallas as pl
from jax.experimental.pallas import tpu as pltpu

F32 = jnp.float32
BF16 = jnp.bfloat16
HI = lax.Precision.HIGHEST

GRID_W = 64
N_MOD = 6
NORM_EPS = 1e-6
LRU_WIDTH = 1024
LRU_BLOCK = 128
LRU_C = 8.0
GDN_HEADS = 8
GDN_DK = 128
GDN_DV = 128
QK_WIDTH = GDN_HEADS * GDN_DK
V_WIDTH = GDN_HEADS * GDN_DV
CHUNK = 64
CONV_W = 4
CONV_PAD_LEFT = 2
CONV_CH = LRU_WIDTH + 2 * QK_WIDTH + V_WIDTH
MAIN_WIDTH = CONV_CH + LRU_WIDTH + V_WIDTH
N_GATE = 4 * GDN_HEADS
TOP_K = 8
TOP_K_LOG2 = 3
ROUTED_SCALE = 2.5

LANE = 128
ROW_TILE = 256
VMEM_LIMIT = 56 * 1024 * 1024

NT_DIMS = (((1,), (1,)), ((), ()))
TN_DIMS = (((0,), (0,)), ((), ()))


def _params(*sem):
    return pltpu.CompilerParams(dimension_semantics=sem, vmem_limit_bytes=VMEM_LIMIT)


def _silu(x):
    return x * jax.nn.sigmoid(x)


def _norm_mod(x, g, shift, scale):
    ms = jnp.mean(x * x, axis=-1, keepdims=True)
    return (x * lax.rsqrt(ms + NORM_EPS) * g) * (1.0 + scale) + shift


def _mod_kernel(s_ref, w_ref, b_ref, o_ref):
    s = _silu(s_ref[...])
    o_ref[...] = jnp.dot(s.astype(BF16), w_ref[...].astype(BF16),
                         preferred_element_type=F32) + b_ref[...]


def _modulation(c, c_ctx, w_mod, b_mod):
    depth, d, nd = w_mod.shape
    b = c.shape[0]
    rows = jnp.concatenate([c, c_ctx[None], jnp.zeros((8 - b - 1, d), F32)], axis=0)
    tn = 1024
    out = pl.pallas_call(
        _mod_kernel,
        name="modulation",
        grid=(depth, nd // tn),
        in_specs=[pl.BlockSpec((8, d), lambda l, j: (0, 0)),
                  pl.BlockSpec((None, d, tn), lambda l, j: (l, 0, j)),
                  pl.BlockSpec((None, 1, tn), lambda l, j: (l, 0, j))],
        out_specs=pl.BlockSpec((None, 8, tn), lambda l, j: (l, 0, j)),
        out_shape=jax.ShapeDtypeStruct((depth, 8, nd), F32),
        compiler_params=_params("arbitrary", "arbitrary"),
    )(rows, w_mod, b_mod.reshape(depth, 1, nd))
    return out.reshape(depth, 8, N_MOD, d)


def _inprep_kernel(x_ref, mod_ref, g_ref, wg_ref, h_ref, pg_ref):
    h = _norm_mod(x_ref[...], g_ref[...], mod_ref[0:1, :], mod_ref[1:2, :])
    hb = h.astype(BF16)
    h_ref[...] = hb
    pg_ref[...] = jnp.dot(hb, wg_ref[...], preferred_element_type=F32)


def _inprep(x, mods, g, w_gatecols, l, n_ctx):
    b, t, d = x.shape
    tm = ROW_TILE
    nct = n_ctx // tm
    return pl.pallas_call(
        _inprep_kernel,
        name="inprep",
        grid=(b, t // tm),
        in_specs=[pl.BlockSpec((None, tm, d), lambda bi, i: (bi, i, 0)),
                  pl.BlockSpec((None, None, N_MOD, d),
                               lambda bi, i: (l, jnp.where(i < nct, b, bi), 0, 0)),
                  pl.BlockSpec((None, 1, d), lambda bi, i: (l, 0, 0)),
                  pl.BlockSpec((d, N_GATE), lambda bi, i: (0, 0))],
        out_specs=[pl.BlockSpec((None, tm, d), lambda bi, i: (bi, i, 0)),
                   pl.BlockSpec((None, tm, N_GATE), lambda bi, i: (bi, i, 0))],
        out_shape=[jax.ShapeDtypeStruct((b, t, d), BF16),
                   jax.ShapeDtypeStruct((b, t, N_GATE), F32)],
        compiler_params=_params("arbitrary", "arbitrary"),
    )(x, mods, g, w_gatecols)


def _matmul_kernel(h_ref, w_ref, o_ref, wb_ref):
    @pl.when(pl.program_id(1) == 0)
    def _():
        wb_ref[...] = w_ref[...].astype(BF16)

    o_ref[...] = jnp.dot(h_ref[...], wb_ref[...], preferred_element_type=F32)


def _inproj(h2d, w_in, l):
    n, d = h2d.shape
    tn, tm = 1024, 512
    return pl.pallas_call(
        _matmul_kernel,
        name="inproj",
        grid=(MAIN_WIDTH // tn, n // tm),
        in_specs=[pl.BlockSpec((tm, d), lambda j, i: (i, 0)),
                  pl.BlockSpec((None, d, tn), lambda j, i: (l, 0, j))],
        out_specs=pl.BlockSpec((tm, tn), lambda j, i: (i, j)),
        out_shape=jax.ShapeDtypeStruct((n, MAIN_WIDTH), F32),
        scratch_shapes=[pltpu.VMEM((d, tn), BF16)],
        compiler_params=_params("arbitrary", "arbitrary"),
    )(h2d, w_in)


def _conv_kernel(p_ref, cw_ref, u_ref, *, n_ctx):
    kind = pl.program_id(1) // GDN_HEADS
    x = p_ref[...]
    t_len = x.shape[0]
    t = lax.broadcasted_iota(jnp.int32, x.shape, 0)
    seg_t = jnp.where(t >= n_ctx, 1, 0)
    acc = jnp.zeros_like(x)
    for j in range(CONV_W):
        off = j - CONV_PAD_LEFT
        if off == 0:
            xs = x
        else:
            xs = pltpu.roll(x, (-off) % t_len, 0)
            s = t + off
            seg_s = jnp.where(s >= n_ctx, 1, 0) + jnp.where(s >= t_len, 1, 0) - jnp.where(s < 0, 1, 0)
            xs = jnp.where(seg_s == seg_t, xs, 0.0)
        acc = acc + xs * cw_ref[j:j + 1, :]

    @pl.when(kind == 0)
    def _():
        u_ref[...] = acc

    @pl.when(kind == 3)
    def _():
        u_ref[...] = _silu(acc)

    @pl.when(jnp.logical_or(kind == 1, kind == 2))
    def _():
        y = _silu(acc)
        nrm = y * lax.rsqrt(jnp.sum(y * y, axis=-1, keepdims=True) + NORM_EPS)
        u_ref[...] = nrm * jnp.where(kind == 1, GDN_DK ** -0.5, 1.0)


def _conv(p, conv_w, l, n_ctx):
    b, t, _ = p.shape
    return pl.pallas_call(
        functools.partial(_conv_kernel, n_ctx=n_ctx),
        name="conv",
        grid=(b, CONV_CH // LANE),
        in_specs=[pl.BlockSpec((None, t, LANE), lambda bi, j: (bi, 0, j)),
                  pl.BlockSpec((None, CONV_W, LANE), lambda bi, j: (l, 0, j))],
        out_specs=pl.BlockSpec((None, t, LANE), lambda bi, j: (bi, 0, j)),
        out_shape=jax.ShapeDtypeStruct((b, t, CONV_CH), F32),
        compiler_params=_params("arbitrary", "arbitrary"),
    )(p, conv_w)


LRU_LANES = 512


def _lru_kernel(u_ref, y_ref, wa_ref, ba_ref, wi_ref, bi_ref, lam_ref, o_ref,
                af_ref, bf_ref, ab_ref, bb_ref, *, n_ctx):
    t_len, width = u_ref.shape
    nblk = width // LRU_BLOCK
    tm = ROW_TILE

    def gates(i, carry):
        r0 = pl.multiple_of(i * tm, tm)
        u = u_ref[pl.ds(r0, tm), :]
        ub = u.astype(BF16)
        for d, (a_ref, b_ref) in enumerate(((af_ref, bf_ref), (ab_ref, bb_ref))):
            pre_a = jnp.concatenate(
                [jnp.dot(ub[:, n * LRU_BLOCK:(n + 1) * LRU_BLOCK], wa_ref[d, n].astype(BF16),
                         preferred_element_type=F32) for n in range(nblk)], axis=-1)
            pre_i = jnp.concatenate(
                [jnp.dot(ub[:, n * LRU_BLOCK:(n + 1) * LRU_BLOCK], wi_ref[d, n].astype(BF16),
                         preferred_element_type=F32) for n in range(nblk)], axis=-1)
            r = jax.nn.sigmoid(pre_a + ba_ref[d:d + 1, :])
            gi = jax.nn.sigmoid(pre_i + bi_ref[d:d + 1, :])
            lam = lam_ref[d:d + 1, :]
            log_sig = jnp.minimum(lam, 0.0) - jnp.log(1.0 + jnp.exp(-jnp.abs(lam)))
            log_a = LRU_C * r * log_sig
            a = jnp.exp(log_a)
            bcoef = jnp.sqrt(jnp.maximum(1.0 - jnp.exp(2.0 * log_a), 0.0)) * (gi * u)
            a_ref[pl.ds(r0, tm), :] = a
            b_ref[pl.ds(r0, tm), :] = bcoef
        return carry

    lax.fori_loop(0, t_len // tm, gates, 0)

    def scan(gi, carry):
        hf, hb = carry
        base = pl.multiple_of(gi * 8, 8)
        bbase = jnp.where(base < n_ctx, n_ctx - 1 - base, t_len + n_ctx - 1 - base)
        for j in range(8):
            tf = base + j
            tb = bbase - j
            hf = af_ref[pl.ds(tf, 1), :] * hf + bf_ref[pl.ds(tf, 1), :]
            af_ref[pl.ds(tf, 1), :] = hf
            hb = ab_ref[pl.ds(tb, 1), :] * hb + bb_ref[pl.ds(tb, 1), :]
            ab_ref[pl.ds(tb, 1), :] = hb
        return hf, hb

    zero = jnp.zeros((1, width), F32)
    lax.fori_loop(0, t_len // 8, scan, (zero, zero))
    o_ref[...] = jax.nn.gelu(y_ref[...], approximate=True) * (af_ref[...] + ab_ref[...])


def _lru(u, p, lru_w_a, lru_b_a, lru_w_i, lru_b_i, lru_lam, l, n_ctx):
    b, t, _ = u.shape
    w = LRU_LANES
    nb = w // LRU_BLOCK
    y_off = CONV_CH // w
    wspec = pl.BlockSpec((None, 2, nb, LRU_BLOCK, LRU_BLOCK), lambda bi, j: (l, 0, j, 0, 0))
    vspec = pl.BlockSpec((None, 2, w), lambda bi, j: (l, 0, j))
    return pl.pallas_call(
        functools.partial(_lru_kernel, n_ctx=n_ctx),
        name="lru",
        grid=(b, LRU_WIDTH // w),
        in_specs=[pl.BlockSpec((None, t, w), lambda bi, j: (bi, 0, j)),
                  pl.BlockSpec((None, t, w), lambda bi, j: (bi, 0, y_off + j)),
                  wspec, vspec, wspec, vspec, vspec],
        out_specs=pl.BlockSpec((None, t, w), lambda bi, j: (bi, 0, j)),
        out_shape=jax.ShapeDtypeStruct((b, t, LRU_WIDTH), F32),
        scratch_shapes=[pltpu.VMEM((t, w), F32)] * 4,
        compiler_params=_params("arbitrary", "arbitrary"),
    )(u, p, lru_w_a, lru_b_a, lru_w_i, lru_b_i, lru_lam)


def _split(a):
    hi = a.astype(BF16).astype(F32)
    return hi, a - hi


def _lhs3(a):
    hi, lo = _split(a)
    return jnp.concatenate([hi, lo, hi], axis=1).astype(BF16)


def _rhs3(b):
    hi, lo = _split(b)
    return jnp.concatenate([hi, hi, lo], axis=0).astype(BF16)


def _gdn_kernel(q_ref, k_ref, v_ref, gt_ref, alog_ref, dtb_ref, o_ref, s_ref, *, reverse):
    c, h_n, dk, dv = CHUNK, GDN_HEADS, GDN_DK, GDN_DV
    d = int(reverse)

    @pl.when(pl.program_id(1) == 0)
    def _():
        s_ref[...] = jnp.zeros_like(s_ref)

    gt = gt_ref[...]
    beta = jax.nn.sigmoid(gt[:, d * h_n:(d + 1) * h_n])
    xa = gt[:, (2 + d) * h_n:(3 + d) * h_n] + dtb_ref[d:d + 1, :]
    softplus = jnp.maximum(xa, 0.0) + jnp.log(1.0 + jnp.exp(-jnp.abs(xa)))
    g = -jnp.exp(alog_ref[d:d + 1, :]) * softplus

    ri = lax.broadcasted_iota(jnp.int32, (c, c), 0)
    ci = lax.broadcasted_iota(jnp.int32, (c, c), 1)
    incl = (ri <= ci) if reverse else (ri >= ci)
    strict = (ri < ci) if reverse else (ri > ci)
    ones_incl = jnp.where(incl, 1.0, 0.0)
    cum = jnp.dot(ones_incl, g, precision=HI, preferred_element_type=F32)
    cum_t = lax.dot_general(g, ones_incl, (((0,), (1,)), ((), ())), precision=HI,
                            preferred_element_type=F32)
    total = cum[0:1, :] if reverse else cum[c - 1:c, :]
    e_cum = jnp.exp(cum)
    e_rest = jnp.exp(total - cum)
    e_total = jnp.exp(total)

    gammas, negs, xs = [], [], []
    for h in range(h_n):
        sl = slice(h * dk, (h + 1) * dk)
        k = k_ref[:, sl]
        kb = k.astype(BF16)
        diff = jnp.where(incl, cum[:, h:h + 1] - cum_t[h:h + 1, :], 0.0)
        gamma = jnp.where(incl, jnp.exp(diff), 0.0)
        bcol = beta[:, h:h + 1]
        kk = lax.dot_general(kb, kb, NT_DIMS, preferred_element_type=F32)
        gammas.append(gamma)
        negs.append(jnp.where(strict, -(bcol * kk * gamma), 0.0))
        xs.append(jnp.concatenate([bcol * v_ref[:, sl], bcol * k * e_cum[:, h:h + 1]], axis=-1))

    for level in range(6):
        lhs = [_lhs3(p) for p in negs]
        xs = [x + jnp.dot(a, _rhs3(x), preferred_element_type=F32) for a, x in zip(lhs, xs)]
        if level < 5:
            negs = [jnp.dot(a, _rhs3(p), preferred_element_type=F32) for a, p in zip(lhs, negs)]

    for h in range(h_n):
        sl = slice(h * dk, (h + 1) * dk)
        q = q_ref[:, sl]
        k = k_ref[:, sl]
        u = xs[h][:, :dv]
        w = xs[h][:, dv:]
        qk = lax.dot_general(q.astype(BF16), k.astype(BF16), NT_DIMS,
                             preferred_element_type=F32) * gammas[h]
        q_dec = q * e_cum[:, h:h + 1]
        k_dec = k * e_rest[:, h:h + 1]
        s = s_ref[h]
        ws = jnp.dot(jnp.concatenate([w, q_dec], axis=0).astype(BF16), s.astype(BF16),
                     preferred_element_type=F32)
        v_new = u - ws[:c]
        vb = v_new.astype(BF16)
        o_ref[:, h * dv:(h + 1) * dv] = ws[c:] + jnp.dot(qk.astype(BF16), vb,
                                                         preferred_element_type=F32)
        s_ref[h] = e_total[:, h:h + 1] * s + lax.dot_general(
            k_dec.astype(BF16), vb, TN_DIMS, preferred_element_type=F32)


def _gdn(u, pg, gdn_a_log, gdn_dt_bias, l, n_ctx, reverse):
    b, t, _ = u.shape
    nc = t // CHUNK
    ncc = n_ctx // CHUNK

    def chunk(s):
        if not reverse:
            return s
        return jnp.where(s < ncc, ncc - 1 - s, nc + ncc - 1 - s)

    def qkv_spec(col):
        return pl.BlockSpec((None, CHUNK, QK_WIDTH), lambda bi, s: (bi, chunk(s), col))

    small = pl.BlockSpec((None, 2, GDN_HEADS), lambda bi, s: (l, 0, 0))
    return pl.pallas_call(
        functools.partial(_gdn_kernel, reverse=reverse),
        name="gdn_bwd" if reverse else "gdn_fwd",
        grid=(b, nc),
        in_specs=[qkv_spec(1), qkv_spec(2), qkv_spec(3),
                  pl.BlockSpec((None, CHUNK, N_GATE), lambda bi, s: (bi, chunk(s), 0)),
                  small, small],
        out_specs=pl.BlockSpec((None, CHUNK, V_WIDTH), lambda bi, s: (bi, chunk(s), 0)),
        out_shape=jax.ShapeDtypeStruct((b, t, V_WIDTH), F32),
        scratch_shapes=[pltpu.VMEM((GDN_HEADS, GDN_DK, GDN_DV), F32)],
        compiler_params=_params("arbitrary", "arbitrary"),
    )(u, u, u, pg, gdn_a_log, gdn_dt_bias)


def _outproj_kernel(ylru_ref, of_ref, ob_ref, z_ref, x_ref, mod_ref, gn_ref, w_ref, o_ref):
    o = of_ref[...] + ob_ref[...]
    z = z_ref[...]
    parts = [ylru_ref[...]]
    for h in range(GDN_HEADS):
        sl = slice(h * GDN_DV, (h + 1) * GDN_DV)
        oh = o[:, sl]
        ms = jnp.mean(oh * oh, axis=-1, keepdims=True)
        parts.append(oh * lax.rsqrt(ms + NORM_EPS) * gn_ref[...] * _silu(z[:, sl]))
    feats = jnp.concatenate(parts, axis=-1).astype(BF16)
    y = jnp.dot(feats, w_ref[...], preferred_element_type=F32)
    o_ref[...] = x_ref[...] + mod_ref[2:3, :] * y


def _outproj(y_lru, o_f, o_b, p, x, mods, gdn_norm_g, w_out_bf, l, n_ctx):
    b, t, d = x.shape
    tm = ROW_TILE
    nct = n_ctx // tm
    z_off = (CONV_CH + LRU_WIDTH) // V_WIDTH
    row = lambda bi, i: (bi, i, 0)
    return pl.pallas_call(
        _outproj_kernel,
        name="outproj",
        grid=(b, t // tm),
        in_specs=[pl.BlockSpec((None, tm, LRU_WIDTH), row),
                  pl.BlockSpec((None, tm, V_WIDTH), row),
                  pl.BlockSpec((None, tm, V_WIDTH), row),
                  pl.BlockSpec((None, tm, V_WIDTH), lambda bi, i: (bi, i, z_off)),
                  pl.BlockSpec((None, tm, d), row),
                  pl.BlockSpec((None, None, N_MOD, d),
                               lambda bi, i: (l, jnp.where(i < nct, b, bi), 0, 0)),
                  pl.BlockSpec((None, 1, GDN_DV), lambda bi, i: (l, 0, 0)),
                  pl.BlockSpec((None, LRU_WIDTH + V_WIDTH, d), lambda bi, i: (l, 0, 0))],
        out_specs=pl.BlockSpec((None, tm, d), row),
        out_shape=jax.ShapeDtypeStruct((b, t, d), F32),
        compiler_params=_params("arbitrary", "arbitrary"),
    )(y_lru, o_f, o_b, p, x, mods, gdn_norm_g, w_out_bf)


def _router_kernel(x_ref, mod_ref, g_ref, wr_ref, rb_ref, h_ref, idx_ref, wt_ref, rank_ref,
                   cnt_ref, run_ref):
    first = jnp.logical_and(pl.program_id(0) == 0, pl.program_id(1) == 0)

    @pl.when(first)
    def _():
        run_ref[...] = jnp.zeros_like(run_ref)

    h = _norm_mod(x_ref[...], g_ref[...], mod_ref[3:4, :], mod_ref[4:5, :])
    scores = jax.nn.sigmoid(jnp.dot(h, wr_ref[...], precision=HI, preferred_element_type=F32))
    tm, n_e = scores.shape
    nb = h.shape[1] // LANE
    for s in range(nb):
        h_ref[pl.ds(s, tm, stride=nb), :] = h[:, s * LANE:(s + 1) * LANE]
    lane = lax.broadcasted_iota(jnp.int32, (tm, n_e), 1)
    kl = lax.broadcasted_iota(jnp.int32, (tm, TOP_K), 1)
    biased = scores + rb_ref[...]
    idx = jnp.zeros((tm, TOP_K), jnp.int32)
    sel = jnp.zeros((tm, TOP_K), F32)
    maskf = jnp.zeros((tm, n_e), F32)
    onehots = []
    for k in range(TOP_K):
        m = jnp.max(biased, axis=-1, keepdims=True)
        ik = jnp.min(jnp.where(biased == m, lane, n_e), axis=-1, keepdims=True)
        oh = lane == ik
        onehots.append(oh)
        sk = jnp.sum(jnp.where(oh, scores, 0.0), axis=-1, keepdims=True)
        idx = jnp.where(kl == k, ik, idx)
        sel = jnp.where(kl == k, sk, sel)
        maskf = jnp.where(oh, 1.0, maskf)
        biased = jnp.where(oh, -jnp.inf, biased)
    wt_ref[...] = sel / jnp.sum(sel, axis=-1, keepdims=True) * ROUTED_SCALE
    idx_ref[...] = idx

    ri = lax.broadcasted_iota(jnp.int32, (tm, tm), 0)
    ci = lax.broadcasted_iota(jnp.int32, (tm, tm), 1)
    before = jnp.where(ri > ci, 1.0, 0.0).astype(BF16)
    prior = jnp.dot(before, maskf.astype(BF16), preferred_element_type=F32) + run_ref[...]
    rank = jnp.zeros((tm, TOP_K), F32)
    for k in range(TOP_K):
        rk = jnp.sum(jnp.where(onehots[k], prior, 0.0), axis=-1, keepdims=True)
        rank = jnp.where(kl == k, rk, rank)
    rank_ref[...] = rank.astype(jnp.int32)
    run_ref[...] = run_ref[...] + jnp.sum(maskf, axis=0, keepdims=True)
    cnt_ref[...] = run_ref[...]


def _router(x, mods, g, w_router, router_bias, l, n_ctx):
    b, t, d = x.shape
    n_e = w_router.shape[-1]
    tm = ROW_TILE
    nt = t // tm
    nct = n_ctx // tm
    nb = d // LANE
    row = lambda bi, i: (bi, i, 0)
    return pl.pallas_call(
        _router_kernel,
        name="router",
        grid=(b, nt),
        in_specs=[pl.BlockSpec((None, tm, d), row),
                  pl.BlockSpec((None, None, N_MOD, d),
                               lambda bi, i: (l, jnp.where(i < nct, b, bi), 0, 0)),
                  pl.BlockSpec((None, 1, d), lambda bi, i: (l, 0, 0)),
                  pl.BlockSpec((None, d, n_e), lambda bi, i: (l, 0, 0)),
                  pl.BlockSpec((None, 1, n_e), lambda bi, i: (l, 0, 0))],
        out_specs=[pl.BlockSpec((tm * nb, LANE), lambda bi, i: (bi * nt + i, 0)),
                   pl.BlockSpec((None, tm, TOP_K), row),
                   pl.BlockSpec((None, tm, TOP_K), row),
                   pl.BlockSpec((None, tm, TOP_K), row),
                   pl.BlockSpec((1, n_e), lambda bi, i: (0, 0))],
        out_shape=[jax.ShapeDtypeStruct((b * t * nb, LANE), F32),
                   jax.ShapeDtypeStruct((b, t, TOP_K), jnp.int32),
                   jax.ShapeDtypeStruct((b, t, TOP_K), F32),
                   jax.ShapeDtypeStruct((b, t, TOP_K), jnp.int32),
                   jax.ShapeDtypeStruct((1, n_e), F32)],
        scratch_shapes=[pltpu.VMEM((1, n_e), F32)],
        compiler_params=_params("arbitrary", "arbitrary"),
    )(x, mods, g, w_router, router_bias)


INVERT_BLOCK = 4096


def _invert_kernel(dest_ref, pad_lo_ref, pad_hi_ref, src_ref, *, n_sel):
    i = pl.program_id(0)

    @pl.when(i == 0)
    def _():
        def fill_range(e, carry):
            def fill(p, c):
                src_ref[p] = n_sel
                return c

            lax.fori_loop(pad_lo_ref[e], pad_hi_ref[e], fill, 0)
            return carry

        lax.fori_loop(0, pad_lo_ref.shape[0], fill_range, 0)

    base = i * INVERT_BLOCK

    def put(r, carry):
        src_ref[dest_ref[r]] = base + r
        return carry

    lax.fori_loop(0, INVERT_BLOCK, put, 0, unroll=8)


def _invert(dest_flat, pad_lo, pad_hi, n_slots):
    n_sel = dest_flat.shape[0]
    whole = pl.BlockSpec(memory_space=pltpu.SMEM)
    return pl.pallas_call(
        functools.partial(_invert_kernel, n_sel=n_sel),
        name="invert",
        grid=(n_sel // INVERT_BLOCK,),
        in_specs=[pl.BlockSpec((INVERT_BLOCK,), lambda i: (i,), memory_space=pltpu.SMEM),
                  whole, whole],
        out_specs=whole,
        out_shape=jax.ShapeDtypeStruct((n_slots,), jnp.int32),
        compiler_params=_params("arbitrary"),
    )(dest_flat, pad_lo, pad_hi)


EXPERT_TILE = 256


def _expert_kernel(te_ref, na_ref, src_ref, h_ref, wg_ref, wu_ref, wd_ref, yk_ref,
                   xbuf, ybuf, wgb_ref, wub_ref, wdb_ref, gsem, ssem, *, nb, n_sel):
    i = pl.program_id(0)
    na = na_ref[0]
    tm = EXPERT_TILE
    rows = tm * nb

    def gather_row(base, r, slot):
        s = src_ref[base + r]
        tok = jnp.where(s < n_sel, lax.shift_right_logical(s, TOP_K_LOG2), 0)
        pltpu.make_async_copy(h_ref.at[pl.ds(pl.multiple_of(tok * nb, nb), nb)],
                              xbuf.at[slot, pl.ds(r * nb, nb)], gsem.at[slot]).start()

    def scatter_row(base, r, slot):
        s = src_ref[base + r]
        dst = jnp.where(s < n_sel, s, n_sel + r)
        pltpu.make_async_copy(ybuf.at[slot, pl.ds(r * nb, nb)],
                              yk_ref.at[pl.ds(pl.multiple_of(dst * nb, nb), nb)],
                              ssem.at[slot]).start()

    def wait_gather(slot):
        pltpu.make_async_copy(h_ref.at[pl.ds(0, rows)], xbuf.at[slot], gsem.at[slot]).wait()

    def wait_scatter(slot):
        pltpu.make_async_copy(ybuf.at[slot], yk_ref.at[pl.ds(0, rows)], ssem.at[slot]).wait()

    @pl.when(i == 0)
    def _():
        ybuf[1] = jnp.zeros((rows, LANE), F32)
        spare = pltpu.make_async_copy(ybuf.at[1], yk_ref.at[pl.ds(n_sel * nb, rows)], ssem.at[1])
        spare.start()
        spare.wait()
        for r in range(tm):
            gather_row(0, r, 0)

    @pl.when(jnp.logical_and(i >= 2, i - 2 < na))
    def _():
        wait_scatter(i % 2)

    @pl.when(i <= na)
    def _():
        wait_gather(i % 2)

    changed = jnp.logical_or(i == 0, te_ref[i] != te_ref[jnp.maximum(i - 1, 0)])

    @pl.when(jnp.logical_and(i < na, changed))
    def _():
        wgb_ref[...] = wg_ref[...].astype(BF16)
        wub_ref[...] = wu_ref[...].astype(BF16)
        wdb_ref[...] = wd_ref[...].astype(BF16)

    def step(p, with_scatter):
        xb = jnp.concatenate([xbuf[p, pl.ds(s, tm, stride=nb), :] for s in range(nb)],
                             axis=-1).astype(BF16)
        hid = _silu(jnp.dot(xb, wgb_ref[...], preferred_element_type=F32)) * jnp.dot(
            xb, wub_ref[...], preferred_element_type=F32)
        y = jnp.dot(hid.astype(BF16), wdb_ref[...], preferred_element_type=F32)
        for s in range(nb):
            ybuf[p, pl.ds(s, tm, stride=nb), :] = y[:, s * LANE:(s + 1) * LANE]
        for r in range(tm):
            gather_row((i + 1) * tm, r, 1 - p)
            if with_scatter:
                scatter_row((i - 1) * tm, r, 1 - p)

    @pl.when(jnp.logical_and(i == 0, i < na))
    def _():
        step(0, False)

    for p in range(2):
        @pl.when(jnp.logical_and(jnp.logical_and(i >= 1, i < na), i % 2 == p))
        def _():
            step(p, True)

    @pl.when(jnp.logical_and(i >= 1, i == na))
    def _():
        def issue(r, carry):
            scatter_row((i - 1) * tm, r, (i - 1) % 2)
            return carry

        lax.fori_loop(0, tm, issue, 0)


def _experts(h_blocks, src, tile_expert, n_active, w_gate, w_up, w_down, l):
    d, f = w_gate.shape[-2:]
    nb = d // LANE
    n_sel = (h_blocks.shape[0] // nb) * TOP_K
    tm = EXPERT_TILE
    n_steps = tile_expert.shape[0]
    assert src.shape[0] == n_steps * tm
    wspec = lambda i, te, na, sr: (l, te[i], 0, 0)
    grid_spec = pltpu.PrefetchScalarGridSpec(
        num_scalar_prefetch=3,
        grid=(n_steps,),
        in_specs=[pl.BlockSpec(memory_space=pl.ANY),
                  pl.BlockSpec((None, None, d, f), wspec),
                  pl.BlockSpec((None, None, d, f), wspec),
                  pl.BlockSpec((None, None, f, d), wspec)],
        out_specs=pl.BlockSpec(memory_space=pl.ANY),
        scratch_shapes=[pltpu.VMEM((2, tm * nb, LANE), F32), pltpu.VMEM((2, tm * nb, LANE), F32),
                        pltpu.VMEM((d, f), BF16), pltpu.VMEM((d, f), BF16),
                        pltpu.VMEM((f, d), BF16),
                        pltpu.SemaphoreType.DMA((2,)), pltpu.SemaphoreType.DMA((2,))],
    )
    return pl.pallas_call(
        functools.partial(_expert_kernel, nb=nb, n_sel=n_sel),
        name="experts",
        grid_spec=grid_spec,
        out_shape=jax.ShapeDtypeStruct(((n_sel + tm) * nb, LANE), F32),
        compiler_params=_params("arbitrary"),
    )(tile_expert, n_active, src, h_blocks, w_gate, w_up, w_down)


COMBINE_TOKENS = 128


def _combine_kernel(wt_ref, yk_ref, x_ref, g_ref, mod_ref, sg_ref, su_ref, sd_ref, o_ref, zb_ref):
    tm, d = x_ref.shape
    nb = d // LANE
    sub = 8

    def per_token(t, carry):
        src = pl.multiple_of(t * (TOP_K * nb), TOP_K * nb)
        dst = pl.multiple_of(t * nb, nb)
        for half in range(nb // sub):
            acc = wt_ref[t * TOP_K] * yk_ref[pl.ds(src + half * sub, sub), :]
            for k in range(1, TOP_K):
                acc = acc + wt_ref[t * TOP_K + k] * yk_ref[pl.ds(src + k * nb + half * sub, sub), :]
            zb_ref[pl.ds(dst + half * sub, sub), :] = acc
        return carry

    lax.fori_loop(0, tm, per_token, 0, unroll=4)
    x = x_ref[...]
    hb = _norm_mod(x, g_ref[...], mod_ref[3:4, :], mod_ref[4:5, :]).astype(BF16)
    hid = _silu(jnp.dot(hb, sg_ref[...], preferred_element_type=F32)) * jnp.dot(
        hb, su_ref[...], preferred_element_type=F32)
    y = jnp.dot(hid.astype(BF16), sd_ref[...], preferred_element_type=F32)
    y = y + jnp.concatenate([zb_ref[pl.ds(s, tm, stride=nb), :] for s in range(nb)], axis=-1)
    o_ref[...] = x + mod_ref[5:6, :] * y


def _combine(yk, x, g, wts, mods, ws_gate_bf, ws_up_bf, ws_down_bf, l, n_ctx):
    b, t, d = x.shape
    tm = COMBINE_TOKENS
    nt = t // tm
    nct = n_ctx // tm
    fs = ws_gate_bf.shape[-1]
    row = lambda bi, i: (bi, i, 0)
    return pl.pallas_call(
        _combine_kernel,
        name="combine",
        grid=(b, nt),
        in_specs=[pl.BlockSpec((tm * TOP_K,), lambda bi, i: (bi * nt + i,),
                               memory_space=pltpu.SMEM),
                  pl.BlockSpec((tm * TOP_K * (d // LANE), LANE), lambda bi, i: (bi * nt + i, 0)),
                  pl.BlockSpec((None, tm, d), row),
                  pl.BlockSpec((None, 1, d), lambda bi, i: (l, 0, 0)),
                  pl.BlockSpec((None, None, N_MOD, d),
                               lambda bi, i: (l, jnp.where(i < nct, b, bi), 0, 0)),
                  pl.BlockSpec((None, d, fs), lambda bi, i: (l, 0, 0)),
                  pl.BlockSpec((None, d, fs), lambda bi, i: (l, 0, 0)),
                  pl.BlockSpec((None, fs, d), lambda bi, i: (l, 0, 0))],
        out_specs=pl.BlockSpec((None, tm, d), row),
        out_shape=jax.ShapeDtypeStruct((b, t, d), F32),
        scratch_shapes=[pltpu.VMEM((tm * (d // LANE), LANE), F32)],
        compiler_params=_params("arbitrary", "arbitrary"),
    )(wts, yk, x, g, mods, ws_gate_bf, ws_up_bf, ws_down_bf)


def _final_kernel(x_ref, g_ref, o_ref):
    x = x_ref[...]
    ms = jnp.mean(x * x, axis=-1, keepdims=True)
    o_ref[...] = x * lax.rsqrt(ms + NORM_EPS) * g_ref[...]


def _final_norm(x, g_final, n_ctx):
    b, t, d = x.shape
    tm = ROW_TILE
    nct = n_ctx // tm
    return pl.pallas_call(
        _final_kernel,
        name="final_norm",
        grid=(b, (t - n_ctx) // tm),
        in_specs=[pl.BlockSpec((None, tm, d), lambda bi, i: (bi, i + nct, 0)),
                  pl.BlockSpec((1, d), lambda bi, i: (0, 0))],
        out_specs=pl.BlockSpec((None, tm, d), lambda bi, i: (bi, i, 0)),
        out_shape=jax.ShapeDtypeStruct((b, t - n_ctx, d), F32),
        compiler_params=_params("arbitrary", "arbitrary"),
    )(x, g_final.reshape(1, d))


def _latent_permute(xs, n_ctx, to_col_major):
    b, t, d = xs.shape
    rows = (t - n_ctx) // GRID_W
    lat = xs[:, n_ctx:]
    if to_col_major:
        lat = lat.reshape(b, rows, GRID_W, d).transpose(0, 2, 1, 3)
    else:
        lat = lat.reshape(b, GRID_W, rows, d).transpose(0, 2, 1, 3)
    return jnp.concatenate([xs[:, :n_ctx], lat.reshape(b, t - n_ctx, d)], axis=1)


def _routing_plan(idx, rank, counts, n_tiles):
    n_e = counts.shape[-1]
    cnt = counts.reshape(n_e).astype(jnp.int32)
    tiles = (cnt + EXPERT_TILE - 1) // EXPERT_TILE
    tile_end = jnp.cumsum(tiles)
    start = (tile_end - tiles) * EXPERT_TILE
    onehot = idx[..., None] == jnp.arange(n_e, dtype=jnp.int32)
    dest = rank + jnp.sum(jnp.where(onehot, start, 0), axis=-1)
    tile_ids = jnp.arange(n_tiles, dtype=jnp.int32)
    tile_expert = jnp.sum(tile_ids[:, None] >= tile_end[None, :], axis=-1).astype(jnp.int32)
    tile_expert = jnp.minimum(tile_expert, n_e - 1)
    n_active = tile_end[-1:].astype(jnp.int32)
    pad_lo = jnp.concatenate([start + cnt, n_active * EXPERT_TILE]).astype(jnp.int32)
    pad_hi = jnp.concatenate([tile_end * EXPERT_TILE,
                              jnp.full((1,), n_tiles * EXPERT_TILE)]).astype(jnp.int32)
    return dest.reshape(-1).astype(jnp.int32), tile_expert, n_active, pad_lo, pad_hi


def kernel(x, c, ctx, c_ctx, w_mod, b_mod, g_mix, g_ffn, w_in, conv_w, lru_w_a, lru_b_a, lru_w_i,
           lru_b_i, lru_lam, gdn_a_log, gdn_dt_bias, gdn_norm_g, w_out, w_router, router_bias,
           w_gate, w_up, w_down, ws_gate, ws_up, ws_down, g_final):
    b, seq, d = x.shape
    n_ctx = ctx.shape[1]
    t = n_ctx + seq
    depth = w_mod.shape[0]
    n_e = w_router.shape[-1]
    assert n_ctx % ROW_TILE == 0 and seq % ROW_TILE == 0 and b < 8
    n_tiles = (b * t * TOP_K) // EXPERT_TILE + n_e + 1

    mods = _modulation(c, c_ctx, w_mod, b_mod)
    g_mix3 = g_mix.reshape(depth, 1, d)
    g_ffn3 = g_ffn.reshape(depth, 1, d)
    gn3 = gdn_norm_g.reshape(depth, 1, GDN_DV)
    rb3 = router_bias.reshape(depth, 1, n_e)
    w_out_bf = w_out.astype(BF16)
    ws_gate_bf, ws_up_bf, ws_down_bf = (w.astype(BF16) for w in (ws_gate, ws_up, ws_down))

    xs = jnp.concatenate([ctx, x], axis=1)
    for l in range(depth):
        col_major = l % 2 == 1
        if col_major:
            xs = _latent_permute(xs, n_ctx, True)
        h, pg = _inprep(xs, mods, g_mix3, w_in[l, :, MAIN_WIDTH:].astype(BF16), l, n_ctx)
        p = _inproj(h.reshape(b * t, d), w_in, l).reshape(b, t, MAIN_WIDTH)
        u = _conv(p, conv_w, l, n_ctx)
        y_lru = _lru(u, p, lru_w_a, lru_b_a, lru_w_i, lru_b_i, lru_lam, l, n_ctx)
        o_f = _gdn(u, pg, gdn_a_log, gdn_dt_bias, l, n_ctx, False)
        o_b = _gdn(u, pg, gdn_a_log, gdn_dt_bias, l, n_ctx, True)
        xs = _outproj(y_lru, o_f, o_b, p, xs, mods, gn3, w_out_bf, l, n_ctx)
        if col_major:
            xs = _latent_permute(xs, n_ctx, False)

        h2, idx, wts, rank, counts = _router(xs, mods, g_ffn3, w_router, rb3, l, n_ctx)
        dest, tile_expert, n_active, pad_lo, pad_hi = _routing_plan(idx, rank, counts, n_tiles)
        src = _invert(dest, pad_lo, pad_hi, n_tiles * EXPERT_TILE)
        yk = _experts(h2, src, tile_expert, n_active, w_gate, w_up, w_down, l)
        xs = _combine(yk, xs, g_ffn3, wts.reshape(-1), mods, ws_gate_bf, ws_up_bf, ws_down_bf, l,
                      n_ctx)
    return _final_norm(xs, g_final, n_ctx)
```

```python
import functools

import jax
import jax.numpy as jnp
from jax import lax
from jax.experimental import pallas as pl
from jax.experimental.pallas import tpu as pltpu

F32 = jnp.float32
BF16 = jnp.bfloat16
HI = lax.Precision.HIGHEST

GRID_W = 64
N_MOD = 6
NORM_EPS = 1e-6
LRU_WIDTH = 1024
LRU_BLOCK = 128
LRU_C = 8.0
GDN_HEADS = 8
GDN_DK = 128
GDN_DV = 128
QK_WIDTH = GDN_HEADS * GDN_DK
V_WIDTH = GDN_HEADS * GDN_DV
CHUNK = 64
CONV_W = 4
CONV_PAD_LEFT = 2
CONV_CH = LRU_WIDTH + 2 * QK_WIDTH + V_WIDTH
MAIN_WIDTH = CONV_CH + LRU_WIDTH + V_WIDTH
N_GATE = 4 * GDN_HEADS
TOP_K = 8
ROUTED_SCALE = 2.5

LANE = 128
ROW_TILE = 256
VMEM_LIMIT = 56 * 1024 * 1024

NT_DIMS = (((1,), (1,)), ((), ()))
TN_DIMS = (((0,), (0,)), ((), ()))


def _params(*sem):
    return pltpu.CompilerParams(dimension_semantics=sem, vmem_limit_bytes=VMEM_LIMIT)


def _silu(x):
    return x * jax.nn.sigmoid(x)


def _norm_mod(x, g, shift, scale):
    ms = jnp.mean(x * x, axis=-1, keepdims=True)
    return (x * lax.rsqrt(ms + NORM_EPS) * g) * (1.0 + scale) + shift


def _to_blocks(y, blk_ref):
    tm, d = y.shape
    nb = d // LANE
    for s in range(nb):
        blk_ref[pl.ds(s, tm, stride=nb), :] = y[:, s * LANE:(s + 1) * LANE]


def _from_blocks(blk_ref, tm):
    nb = blk_ref.shape[0] // tm
    return jnp.concatenate([blk_ref[pl.ds(s, tm, stride=nb), :] for s in range(nb)], axis=-1)


def _mod_kernel(s_ref, w_ref, b_ref, o_ref):
    s = _silu(s_ref[...])
    o_ref[...] = jnp.dot(s.astype(BF16), w_ref[...].astype(BF16),
                         preferred_element_type=F32) + b_ref[...]


def _modulation(c, c_ctx, w_mod, b_mod):
    depth, d, nd = w_mod.shape
    b = c.shape[0]
    rows = jnp.concatenate([c, c_ctx[None], jnp.zeros((8 - b - 1, d), F32)], axis=0)
    tn = 1024
    out = pl.pallas_call(
        _mod_kernel,
        name="modulation",
        grid=(depth, nd // tn),
        in_specs=[pl.BlockSpec((8, d), lambda l, j: (0, 0)),
                  pl.BlockSpec((None, d, tn), lambda l, j: (l, 0, j)),
                  pl.BlockSpec((None, 1, tn), lambda l, j: (l, 0, j))],
        out_specs=pl.BlockSpec((None, 8, tn), lambda l, j: (l, 0, j)),
        out_shape=jax.ShapeDtypeStruct((depth, 8, nd), F32),
        compiler_params=_params("arbitrary", "arbitrary"),
    )(rows, w_mod, b_mod.reshape(depth, 1, nd))
    return out.reshape(depth, 8, N_MOD, d)


def _inprep_kernel(x_ref, mod_ref, g_ref, wg_ref, h_ref, pg_ref):
    h = _norm_mod(x_ref[...], g_ref[...], mod_ref[0:1, :], mod_ref[1:2, :])
    hb = h.astype(BF16)
    h_ref[...] = hb
    pg_ref[...] = jnp.dot(hb, wg_ref[...], preferred_element_type=F32)


def _inprep(x, mods, g, w_gatecols, l, n_ctx):
    b, t, d = x.shape
    tm = ROW_TILE
    nct = n_ctx // tm
    return pl.pallas_call(
        _inprep_kernel,
        name="inprep",
        grid=(b, t // tm),
        in_specs=[pl.BlockSpec((None, tm, d), lambda bi, i: (bi, i, 0)),
                  pl.BlockSpec((None, None, N_MOD, d),
                               lambda bi, i: (l, jnp.where(i < nct, b, bi), 0, 0)),
                  pl.BlockSpec((None, 1, d), lambda bi, i: (l, 0, 0)),
                  pl.BlockSpec((d, N_GATE), lambda bi, i: (0, 0))],
        out_specs=[pl.BlockSpec((None, tm, d), lambda bi, i: (bi, i, 0)),
                   pl.BlockSpec((None, tm, N_GATE), lambda bi, i: (bi, i, 0))],
        out_shape=[jax.ShapeDtypeStruct((b, t, d), BF16),
                   jax.ShapeDtypeStruct((b, t, N_GATE), F32)],
        compiler_params=_params("arbitrary", "arbitrary"),
    )(x, mods, g, w_gatecols)


def _matmul_kernel(h_ref, w_ref, o_ref, wb_ref):
    @pl.when(pl.program_id(1) == 0)
    def _():
        wb_ref[...] = w_ref[...].astype(BF16)

    o_ref[...] = jnp.dot(h_ref[...], wb_ref[...], preferred_element_type=F32)


def _inproj(h2d, w_in, l):
    n, d = h2d.shape
    tn, tm = 1024, 512
    return pl.pallas_call(
        _matmul_kernel,
        name="inproj",
        grid=(MAIN_WIDTH // tn, n // tm),
        in_specs=[pl.BlockSpec((tm, d), lambda j, i: (i, 0)),
                  pl.BlockSpec((None, d, tn), lambda j, i: (l, 0, j))],
        out_specs=pl.BlockSpec((tm, tn), lambda j, i: (i, j)),
        out_shape=jax.ShapeDtypeStruct((n, MAIN_WIDTH), F32),
        scratch_shapes=[pltpu.VMEM((d, tn), BF16)],
        compiler_params=_params("arbitrary", "arbitrary"),
    )(h2d, w_in)


def _conv_kernel(p_ref, cw_ref, u_ref, *, n_ctx):
    kind = pl.program_id(1) // GDN_HEADS
    x = p_ref[...]
    t_len = x.shape[0]
    t = lax.broadcasted_iota(jnp.int32, x.shape, 0)
    seg_t = jnp.where(t >= n_ctx, 1, 0)
    acc = jnp.zeros_like(x)
    for j in range(CONV_W):
        off = j - CONV_PAD_LEFT
        if off == 0:
            xs = x
        else:
            xs = pltpu.roll(x, (-off) % t_len, 0)
            s = t + off
            seg_s = jnp.where(s >= n_ctx, 1, 0) + jnp.where(s >= t_len, 1, 0) - jnp.where(s < 0, 1, 0)
            xs = jnp.where(seg_s == seg_t, xs, 0.0)
        acc = acc + xs * cw_ref[j:j + 1, :]

    @pl.when(kind == 0)
    def _():
        u_ref[...] = acc

    @pl.when(kind == 3)
    def _():
        u_ref[...] = _silu(acc)

    @pl.when(jnp.logical_or(kind == 1, kind == 2))
    def _():
        y = _silu(acc)
        nrm = y * lax.rsqrt(jnp.sum(y * y, axis=-1, keepdims=True) + NORM_EPS)
        u_ref[...] = nrm * jnp.where(kind == 1, GDN_DK ** -0.5, 1.0)


def _conv(p, conv_w, l, n_ctx):
    b, t, _ = p.shape
    return pl.pallas_call(
        functools.partial(_conv_kernel, n_ctx=n_ctx),
        name="conv",
        grid=(b, CONV_CH // LANE),
        in_specs=[pl.BlockSpec((None, t, LANE), lambda bi, j: (bi, 0, j)),
                  pl.BlockSpec((None, CONV_W, LANE), lambda bi, j: (l, 0, j))],
        out_specs=pl.BlockSpec((None, t, LANE), lambda bi, j: (bi, 0, j)),
        out_shape=jax.ShapeDtypeStruct((b, t, CONV_CH), F32),
        compiler_params=_params("arbitrary", "arbitrary"),
    )(p, conv_w)


LRU_LANES = 512


def _lru_kernel(u_ref, y_ref, wa_ref, ba_ref, wi_ref, bi_ref, lam_ref, o_ref,
                af_ref, bf_ref, ab_ref, bb_ref, *, n_ctx):
    t_len, width = u_ref.shape
    nblk = width // LRU_BLOCK
    tm = ROW_TILE

    def gates(i, carry):
        r0 = pl.multiple_of(i * tm, tm)
        u = u_ref[pl.ds(r0, tm), :]
        ub = u.astype(BF16)
        for d, (a_ref, b_ref) in enumerate(((af_ref, bf_ref), (ab_ref, bb_ref))):
            pre_a = jnp.concatenate(
                [jnp.dot(ub[:, n * LRU_BLOCK:(n + 1) * LRU_BLOCK], wa_ref[d, n].astype(BF16),
                         preferred_element_type=F32) for n in range(nblk)], axis=-1)
            pre_i = jnp.concatenate(
                [jnp.dot(ub[:, n * LRU_BLOCK:(n + 1) * LRU_BLOCK], wi_ref[d, n].astype(BF16),
                         preferred_element_type=F32) for n in range(nblk)], axis=-1)
            r = jax.nn.sigmoid(pre_a + ba_ref[d:d + 1, :])
            gi = jax.nn.sigmoid(pre_i + bi_ref[d:d + 1, :])
            lam = lam_ref[d:d + 1, :]
            log_sig = jnp.minimum(lam, 0.0) - jnp.log(1.0 + jnp.exp(-jnp.abs(lam)))
            log_a = LRU_C * r * log_sig
            a = jnp.exp(log_a)
            bcoef = jnp.sqrt(jnp.maximum(1.0 - jnp.exp(2.0 * log_a), 0.0)) * (gi * u)
            a_ref[pl.ds(r0, tm), :] = a
            b_ref[pl.ds(r0, tm), :] = bcoef
        return carry

    lax.fori_loop(0, t_len // tm, gates, 0)

    def scan(gi, carry):
        hf, hb = carry
        base = pl.multiple_of(gi * 8, 8)
        bbase = jnp.where(base < n_ctx, n_ctx - 1 - base, t_len + n_ctx - 1 - base)
        for j in range(8):
            tf = base + j
            tb = bbase - j
            hf = af_ref[pl.ds(tf, 1), :] * hf + bf_ref[pl.ds(tf, 1), :]
            af_ref[pl.ds(tf, 1), :] = hf
            hb = ab_ref[pl.ds(tb, 1), :] * hb + bb_ref[pl.ds(tb, 1), :]
            ab_ref[pl.ds(tb, 1), :] = hb
        return hf, hb

    zero = jnp.zeros((1, width), F32)
    lax.fori_loop(0, t_len // 8, scan, (zero, zero))
    o_ref[...] = jax.nn.gelu(y_ref[...], approximate=True) * (af_ref[...] + ab_ref[...])


def _lru(u, p, lru_w_a, lru_b_a, lru_w_i, lru_b_i, lru_lam, l, n_ctx):
    b, t, _ = u.shape
    w = LRU_LANES
    nb = w // LRU_BLOCK
    y_off = CONV_CH // w
    wspec = pl.BlockSpec((None, 2, nb, LRU_BLOCK, LRU_BLOCK), lambda bi, j: (l, 0, j, 0, 0))
    vspec = pl.BlockSpec((None, 2, w), lambda bi, j: (l, 0, j))
    return pl.pallas_call(
        functools.partial(_lru_kernel, n_ctx=n_ctx),
        name="lru",
        grid=(b, LRU_WIDTH // w),
        in_specs=[pl.BlockSpec((None, t, w), lambda bi, j: (bi, 0, j)),
                  pl.BlockSpec((None, t, w), lambda bi, j: (bi, 0, y_off + j)),
                  wspec, vspec, wspec, vspec, vspec],
        out_specs=pl.BlockSpec((None, t, w), lambda bi, j: (bi, 0, j)),
        out_shape=jax.ShapeDtypeStruct((b, t, LRU_WIDTH), F32),
        scratch_shapes=[pltpu.VMEM((t, w), F32)] * 4,
        compiler_params=_params("arbitrary", "arbitrary"),
    )(u, p, lru_w_a, lru_b_a, lru_w_i, lru_b_i, lru_lam)


def _split(a):
    hi = a.astype(BF16).astype(F32)
    return hi, a - hi


def _lhs3(a):
    hi, lo = _split(a)
    return jnp.concatenate([hi, lo, hi], axis=1).astype(BF16)


def _rhs3(b):
    hi, lo = _split(b)
    return jnp.concatenate([hi, hi, lo], axis=0).astype(BF16)


def _gdn_kernel(qf_ref, kf_ref, vf_ref, gf_ref, qb_ref, kb_ref, vb_ref, gb_ref, alog_ref, dtb_ref,
                of_ref, ob_ref, s_ref):
    c, h_n, dk, dv = CHUNK, GDN_HEADS, GDN_DK, GDN_DV

    @pl.when(pl.program_id(1) == 0)
    def _():
        s_ref[...] = jnp.zeros_like(s_ref)

    ri = lax.broadcasted_iota(jnp.int32, (c, c), 0)
    ci = lax.broadcasted_iota(jnp.int32, (c, c), 1)
    dirs = ((qf_ref, kf_ref, vf_ref, gf_ref, of_ref), (qb_ref, kb_ref, vb_ref, gb_ref, ob_ref))

    chains = []
    for d, (q_ref, k_ref, v_ref, gt_ref, o_ref) in enumerate(dirs):
        gt = gt_ref[...]
        beta = jax.nn.sigmoid(gt[:, d * h_n:(d + 1) * h_n])
        xa = gt[:, (2 + d) * h_n:(3 + d) * h_n] + dtb_ref[d:d + 1, :]
        softplus = jnp.maximum(xa, 0.0) + jnp.log(1.0 + jnp.exp(-jnp.abs(xa)))
        g = -jnp.exp(alog_ref[d:d + 1, :]) * softplus
        incl = (ri <= ci) if d else (ri >= ci)
        strict = (ri < ci) if d else (ri > ci)
        ones_incl = jnp.where(incl, 1.0, 0.0)
        cum = jnp.dot(ones_incl, g, precision=HI, preferred_element_type=F32)
        cum_t = lax.dot_general(g, ones_incl, (((0,), (1,)), ((), ())), precision=HI,
                                preferred_element_type=F32)
        total = cum[0:1, :] if d else cum[c - 1:c, :]
        e_cum = jnp.exp(cum)
        e_rest = jnp.exp(total - cum)
        e_total = jnp.exp(total)
        for h in range(h_n):
            sl = slice(h * dk, (h + 1) * dk)
            k = k_ref[:, sl]
            kb = k.astype(BF16)
            diff = jnp.where(incl, cum[:, h:h + 1] - cum_t[h:h + 1, :], 0.0)
            gamma = jnp.where(incl, jnp.exp(diff), 0.0)
            bcol = beta[:, h:h + 1]
            kk = lax.dot_general(kb, kb, NT_DIMS, preferred_element_type=F32)
            chains.append(dict(
                d=d, h=h, q_ref=q_ref, k_ref=k_ref, o_ref=o_ref, gamma=gamma,
                neg=jnp.where(strict, -(bcol * kk * gamma), 0.0),
                x=jnp.concatenate([bcol * v_ref[:, sl], bcol * k * e_cum[:, h:h + 1]], axis=-1),
                e_cum=e_cum[:, h:h + 1], e_rest=e_rest[:, h:h + 1], e_total=e_total[:, h:h + 1]))

    for level in range(6):
        for ch in chains:
            a3 = _lhs3(ch["neg"])
            ch["x"] = ch["x"] + jnp.dot(a3, _rhs3(ch["x"]), preferred_element_type=F32)
            if level < 5:
                ch["neg"] = jnp.dot(a3, _rhs3(ch["neg"]), preferred_element_type=F32)

    for ch in chains:
        d, h = ch["d"], ch["h"]
        sl = slice(h * dk, (h + 1) * dk)
        q = ch["q_ref"][:, sl]
        k = ch["k_ref"][:, sl]
        u = ch["x"][:, :dv]
        w = ch["x"][:, dv:]
        qk = lax.dot_general(q.astype(BF16), k.astype(BF16), NT_DIMS,
                             preferred_element_type=F32) * ch["gamma"]
        q_dec = q * ch["e_cum"]
        k_dec = k * ch["e_rest"]
        s = s_ref[d, h]
        ws = jnp.dot(jnp.concatenate([w, q_dec], axis=0).astype(BF16), s.astype(BF16),
                     preferred_element_type=F32)
        v_new = u - ws[:c]
        vb = v_new.astype(BF16)
        ch["o_ref"][:, h * dv:(h + 1) * dv] = ws[c:] + jnp.dot(qk.astype(BF16), vb,
                                                               preferred_element_type=F32)
        s_ref[d, h] = ch["e_total"] * s + lax.dot_general(
            k_dec.astype(BF16), vb, TN_DIMS, preferred_element_type=F32)


def _gdn(u, pg, gdn_a_log, gdn_dt_bias, l, n_ctx):
    b, t, _ = u.shape
    nc = t // CHUNK
    ncc = n_ctx // CHUNK

    def rev(s):
        return jnp.where(s < ncc, ncc - 1 - s, nc + ncc - 1 - s)

    def specs(chunk):
        qkv = [pl.BlockSpec((None, CHUNK, QK_WIDTH), lambda bi, s, col=col: (bi, chunk(s), col))
               for col in (1, 2, 3)]
        return qkv + [pl.BlockSpec((None, CHUNK, N_GATE), lambda bi, s: (bi, chunk(s), 0))]

    fwd = lambda s: s
    small = pl.BlockSpec((None, 2, GDN_HEADS), lambda bi, s: (l, 0, 0))
    out = jax.ShapeDtypeStruct((b, t, V_WIDTH), F32)
    return pl.pallas_call(
        _gdn_kernel,
        name="gdn",
        grid=(b, nc),
        in_specs=specs(fwd) + specs(rev) + [small, small],
        out_specs=[pl.BlockSpec((None, CHUNK, V_WIDTH), lambda bi, s: (bi, s, 0)),
                   pl.BlockSpec((None, CHUNK, V_WIDTH), lambda bi, s: (bi, rev(s), 0))],
        out_shape=[out, out],
        scratch_shapes=[pltpu.VMEM((2, GDN_HEADS, GDN_DK, GDN_DV), F32)],
        compiler_params=_params("arbitrary", "arbitrary"),
    )(u, u, u, pg, u, u, u, pg, gdn_a_log, gdn_dt_bias)


def _outproj_kernel(ylru_ref, of_ref, ob_ref, z_ref, x_ref, mod_ref, gn_ref, w_ref, o_ref):
    o = of_ref[...] + ob_ref[...]
    z = z_ref[...]
    parts = [ylru_ref[...]]
    for h in range(GDN_HEADS):
        sl = slice(h * GDN_DV, (h + 1) * GDN_DV)
        oh = o[:, sl]
        ms = jnp.mean(oh * oh, axis=-1, keepdims=True)
        parts.append(oh * lax.rsqrt(ms + NORM_EPS) * gn_ref[...] * _silu(z[:, sl]))
    feats = jnp.concatenate(parts, axis=-1).astype(BF16)
    y = jnp.dot(feats, w_ref[...], preferred_element_type=F32)
    o_ref[...] = x_ref[...] + mod_ref[2:3, :] * y


def _outproj(y_lru, o_f, o_b, p, x, mods, gdn_norm_g, w_out_bf, l, n_ctx):
    b, t, d = x.shape
    tm = ROW_TILE
    nct = n_ctx // tm
    z_off = (CONV_CH + LRU_WIDTH) // V_WIDTH
    row = lambda bi, i: (bi, i, 0)
    return pl.pallas_call(
        _outproj_kernel,
        name="outproj",
        grid=(b, t // tm),
        in_specs=[pl.BlockSpec((None, tm, LRU_WIDTH), row),
                  pl.BlockSpec((None, tm, V_WIDTH), row),
                  pl.BlockSpec((None, tm, V_WIDTH), row),
                  pl.BlockSpec((None, tm, V_WIDTH), lambda bi, i: (bi, i, z_off)),
                  pl.BlockSpec((None, tm, d), row),
                  pl.BlockSpec((None, None, N_MOD, d),
                               lambda bi, i: (l, jnp.where(i < nct, b, bi), 0, 0)),
                  pl.BlockSpec((None, 1, GDN_DV), lambda bi, i: (l, 0, 0)),
                  pl.BlockSpec((None, LRU_WIDTH + V_WIDTH, d), lambda bi, i: (l, 0, 0))],
        out_specs=pl.BlockSpec((None, tm, d), row),
        out_shape=jax.ShapeDtypeStruct((b, t, d), F32),
        compiler_params=_params("arbitrary", "arbitrary"),
    )(y_lru, o_f, o_b, p, x, mods, gdn_norm_g, w_out_bf)


def _router_kernel(x_ref, mod_ref, g_ref, wr_ref, rb_ref, h_ref, idx_ref, wt_ref, rank_ref,
                   cnt_ref, run_ref, blk_ref):
    first = jnp.logical_and(pl.program_id(0) == 0, pl.program_id(1) == 0)

    @pl.when(first)
    def _():
        run_ref[...] = jnp.zeros_like(run_ref)

    h = _norm_mod(x_ref[...], g_ref[...], mod_ref[3:4, :], mod_ref[4:5, :])
    scores = jax.nn.sigmoid(jnp.dot(h, wr_ref[...], precision=HI, preferred_element_type=F32))
    tm, n_e = scores.shape
    _to_blocks(h, blk_ref)
    h_ref[...] = blk_ref[...].astype(BF16)
    lane = lax.broadcasted_iota(jnp.int32, (tm, n_e), 1)
    kl = lax.broadcasted_iota(jnp.int32, (tm, TOP_K), 1)
    biased = scores + rb_ref[...]
    idx = jnp.zeros((tm, TOP_K), jnp.int32)
    sel = jnp.zeros((tm, TOP_K), F32)
    maskf = jnp.zeros((tm, n_e), F32)
    onehots = []
    for k in range(TOP_K):
        m = jnp.max(biased, axis=-1, keepdims=True)
        ik = jnp.min(jnp.where(biased == m, lane, n_e), axis=-1, keepdims=True)
        oh = lane == ik
        onehots.append(oh)
        sk = jnp.sum(jnp.where(oh, scores, 0.0), axis=-1, keepdims=True)
        idx = jnp.where(kl == k, ik, idx)
        sel = jnp.where(kl == k, sk, sel)
        maskf = jnp.where(oh, 1.0, maskf)
        biased = jnp.where(oh, -jnp.inf, biased)
    wt_ref[...] = sel / jnp.sum(sel, axis=-1, keepdims=True) * ROUTED_SCALE
    idx_ref[...] = idx

    ri = lax.broadcasted_iota(jnp.int32, (tm, tm), 0)
    ci = lax.broadcasted_iota(jnp.int32, (tm, tm), 1)
    before = jnp.where(ri > ci, 1.0, 0.0).astype(BF16)
    prior = jnp.dot(before, maskf.astype(BF16), preferred_element_type=F32) + run_ref[...]
    rank = jnp.zeros((tm, TOP_K), F32)
    for k in range(TOP_K):
        rk = jnp.sum(jnp.where(onehots[k], prior, 0.0), axis=-1, keepdims=True)
        rank = jnp.where(kl == k, rk, rank)
    rank_ref[...] = rank.astype(jnp.int32)
    run_ref[...] = run_ref[...] + jnp.sum(maskf, axis=0, keepdims=True)
    cnt_ref[...] = run_ref[...]


def _router(x, mods, g, w_router, router_bias, l, n_ctx):
    b, t, d = x.shape
    n_e = w_router.shape[-1]
    tm = ROW_TILE
    nt = t // tm
    nct = n_ctx // tm
    nb = d // LANE
    row = lambda bi, i: (bi, i, 0)
    return pl.pallas_call(
        _router_kernel,
        name="router",
        grid=(b, nt),
        in_specs=[pl.BlockSpec((None, tm, d), row),
                  pl.BlockSpec((None, None, N_MOD, d),
                               lambda bi, i: (l, jnp.where(i < nct, b, bi), 0, 0)),
                  pl.BlockSpec((None, 1, d), lambda bi, i: (l, 0, 0)),
                  pl.BlockSpec((None, d, n_e), lambda bi, i: (l, 0, 0)),
                  pl.BlockSpec((None, 1, n_e), lambda bi, i: (l, 0, 0))],
        out_specs=[pl.BlockSpec((tm * nb, LANE), lambda bi, i: (bi * nt + i, 0)),
                   pl.BlockSpec((None, tm, TOP_K), row),
                   pl.BlockSpec((None, tm, TOP_K), row),
                   pl.BlockSpec((None, tm, TOP_K), row),
                   pl.BlockSpec((1, n_e), lambda bi, i: (0, 0))],
        out_shape=[jax.ShapeDtypeStruct((b * t * nb, LANE), BF16),
                   jax.ShapeDtypeStruct((b, t, TOP_K), jnp.int32),
                   jax.ShapeDtypeStruct((b, t, TOP_K), F32),
                   jax.ShapeDtypeStruct((b, t, TOP_K), jnp.int32),
                   jax.ShapeDtypeStruct((1, n_e), F32)],
        scratch_shapes=[pltpu.VMEM((1, n_e), F32), pltpu.VMEM((tm * nb, LANE), F32)],
        compiler_params=_params("arbitrary", "arbitrary"),
    )(x, mods, g, w_router, router_bias)


DISPATCH_TOKENS = 256


def _dispatch_kernel(dest_ref, h_ref, xs_in_ref, xs_ref, sem, *, nb):
    del xs_in_ref

    def issue(r, carry):
        src = h_ref.at[pl.ds(pl.multiple_of(r * nb, nb), nb)]
        for k in range(TOP_K):
            slot = dest_ref[r * TOP_K + k]
            pltpu.make_async_copy(src, xs_ref.at[pl.ds(pl.multiple_of(slot * nb, nb), nb)],
                                  sem).start()
        return carry

    lax.fori_loop(0, DISPATCH_TOKENS, issue, 0)
    for k in range(TOP_K):
        pltpu.make_async_copy(h_ref, xs_ref.at[pl.ds(0, DISPATCH_TOKENS * nb)], sem).wait()


def _dispatch(h_blocks, dest_flat, n_slots, nb):
    n = h_blocks.shape[0] // nb
    xs0 = jnp.zeros((n_slots * nb, LANE), h_blocks.dtype)
    return pl.pallas_call(
        functools.partial(_dispatch_kernel, nb=nb),
        name="dispatch",
        grid=(n // DISPATCH_TOKENS,),
        in_specs=[pl.BlockSpec((DISPATCH_TOKENS * TOP_K,), lambda i: (i,),
                               memory_space=pltpu.SMEM),
                  pl.BlockSpec((DISPATCH_TOKENS * nb, LANE), lambda i: (i, 0)),
                  pl.BlockSpec(memory_space=pl.ANY)],
        out_specs=pl.BlockSpec(memory_space=pl.ANY),
        out_shape=jax.ShapeDtypeStruct((n_slots * nb, LANE), h_blocks.dtype),
        scratch_shapes=[pltpu.SemaphoreType.DMA(())],
        input_output_aliases={2: 0},
        compiler_params=_params("arbitrary"),
    )(dest_flat, h_blocks, xs0)


EXPERT_TILE = 256


def _expert_kernel(te_ref, na_ref, x_ref, wg_ref, wu_ref, wd_ref, y_ref, wgb_ref, wub_ref, wdb_ref,
                   blk_ref):
    i = pl.program_id(0)
    active = i < na_ref[0]
    changed = jnp.logical_or(i == 0, te_ref[i] != te_ref[jnp.maximum(i - 1, 0)])

    @pl.when(jnp.logical_and(active, changed))
    def _():
        wgb_ref[...] = wg_ref[...].astype(BF16)
        wub_ref[...] = wu_ref[...].astype(BF16)
        wdb_ref[...] = wd_ref[...].astype(BF16)

    @pl.when(active)
    def _():
        blk_ref[...] = x_ref[...].astype(F32)
        xb = _from_blocks(blk_ref, EXPERT_TILE).astype(BF16)
        hid = _silu(jnp.dot(xb, wgb_ref[...], preferred_element_type=F32)) * jnp.dot(
            xb, wub_ref[...], preferred_element_type=F32)
        _to_blocks(jnp.dot(hid.astype(BF16), wdb_ref[...], preferred_element_type=F32), blk_ref)
        y_ref[...] = blk_ref[...].astype(BF16)

    @pl.when(jnp.logical_not(active))
    def _():
        y_ref[...] = jnp.zeros_like(y_ref)


def _experts(xs, tile_expert, n_active, w_gate, w_up, w_down, l):
    d, f = w_gate.shape[-2:]
    nb = d // LANE
    n_slots = xs.shape[0] // nb
    tm = EXPERT_TILE
    grid_spec = pltpu.PrefetchScalarGridSpec(
        num_scalar_prefetch=2,
        grid=(n_slots // tm,),
        in_specs=[pl.BlockSpec((tm * nb, LANE), lambda i, te, na: (i, 0)),
                  pl.BlockSpec((None, None, d, f), lambda i, te, na: (l, te[i], 0, 0)),
                  pl.BlockSpec((None, None, d, f), lambda i, te, na: (l, te[i], 0, 0)),
                  pl.BlockSpec((None, None, f, d), lambda i, te, na: (l, te[i], 0, 0))],
        out_specs=pl.BlockSpec((tm * nb, LANE), lambda i, te, na: (i, 0)),
        scratch_shapes=[pltpu.VMEM((d, f), BF16), pltpu.VMEM((d, f), BF16),
                        pltpu.VMEM((f, d), BF16), pltpu.VMEM((tm * nb, LANE), F32)],
    )
    return pl.pallas_call(
        _expert_kernel,
        name="experts",
        grid_spec=grid_spec,
        out_shape=jax.ShapeDtypeStruct((n_slots * nb, LANE), xs.dtype),
        compiler_params=_params("arbitrary"),
    )(tile_expert, n_active, xs, w_gate, w_up, w_down)


COMBINE_TOKENS = 128


def _combine_kernel(dest_ref, wt_ref, ys_ref, x_ref, g_ref, mod_ref, sg_ref, su_ref, sd_ref,
                    o_ref, buf_ref, zb_ref, sem):
    tm, d = x_ref.shape
    nb = d // LANE

    def issue(r, carry):
        for k in range(TOP_K):
            slot = dest_ref[r * TOP_K + k]
            pltpu.make_async_copy(
                ys_ref.at[pl.ds(pl.multiple_of(slot * nb, nb), nb)],
                buf_ref.at[pl.ds(pl.multiple_of((r * TOP_K + k) * nb, nb), nb)], sem).start()
        return carry

    lax.fori_loop(0, tm, issue, 0)
    x = x_ref[...]
    hb = _norm_mod(x, g_ref[...], mod_ref[3:4, :], mod_ref[4:5, :]).astype(BF16)
    hid = _silu(jnp.dot(hb, sg_ref[...], preferred_element_type=F32)) * jnp.dot(
        hb, su_ref[...], preferred_element_type=F32)
    y = jnp.dot(hid.astype(BF16), sd_ref[...], preferred_element_type=F32)
    pltpu.make_async_copy(ys_ref.at[pl.ds(0, TOP_K * tm * nb)], buf_ref, sem).wait()

    def per_token(t, carry):
        src = pl.multiple_of(t * (TOP_K * nb), TOP_K * nb)
        acc = wt_ref[t * TOP_K] * buf_ref[pl.ds(src, nb), :].astype(F32)
        for k in range(1, TOP_K):
            acc = acc + wt_ref[t * TOP_K + k] * buf_ref[pl.ds(src + k * nb, nb), :].astype(F32)
        zb_ref[pl.ds(pl.multiple_of(t * nb, nb), nb), :] = acc
        return carry

    lax.fori_loop(0, tm, per_token, 0, unroll=4)
    o_ref[...] = x + mod_ref[5:6, :] * (y + _from_blocks(zb_ref, tm))


def _combine(ys, dest_flat, wts_flat, x, g, mods, ws_gate_bf, ws_up_bf, ws_down_bf, l, n_ctx):
    b, t, d = x.shape
    tm = COMBINE_TOKENS
    nt = t // tm
    nct = n_ctx // tm
    nb = d // LANE
    fs = ws_gate_bf.shape[-1]
    row = lambda bi, i: (bi, i, 0)
    sel = pl.BlockSpec((tm * TOP_K,), lambda bi, i: (bi * nt + i,), memory_space=pltpu.SMEM)
    return pl.pallas_call(
        _combine_kernel,
        name="combine",
        grid=(b, nt),
        in_specs=[sel, sel,
                  pl.BlockSpec(memory_space=pl.ANY),
                  pl.BlockSpec((None, tm, d), row),
                  pl.BlockSpec((None, 1, d), lambda bi, i: (l, 0, 0)),
                  pl.BlockSpec((None, None, N_MOD, d),
                               lambda bi, i: (l, jnp.where(i < nct, b, bi), 0, 0)),
                  pl.BlockSpec((None, d, fs), lambda bi, i: (l, 0, 0)),
                  pl.BlockSpec((None, d, fs), lambda bi, i: (l, 0, 0)),
                  pl.BlockSpec((None, fs, d), lambda bi, i: (l, 0, 0))],
        out_specs=pl.BlockSpec((None, tm, d), row),
        out_shape=jax.ShapeDtypeStruct((b, t, d), F32),
        scratch_shapes=[pltpu.VMEM((TOP_K * tm * nb, LANE), ys.dtype),
                        pltpu.VMEM((tm * nb, LANE), F32), pltpu.SemaphoreType.DMA(())],
        compiler_params=_params("arbitrary", "arbitrary"),
    )(dest_flat, wts_flat, ys, x, g, mods, ws_gate_bf, ws_up_bf, ws_down_bf)


def _final_kernel(x_ref, g_ref, o_ref):
    x = x_ref[...]
    ms = jnp.mean(x * x, axis=-1, keepdims=True)
    o_ref[...] = x * lax.rsqrt(ms + NORM_EPS) * g_ref[...]


def _final_norm(x, g_final, n_ctx):
    b, t, d = x.shape
    tm = ROW_TILE
    nct = n_ctx // tm
    return pl.pallas_call(
        _final_kernel,
        name="final_norm",
        grid=(b, (t - n_ctx) // tm),
        in_specs=[pl.BlockSpec((None, tm, d), lambda bi, i: (bi, i + nct, 0)),
                  pl.BlockSpec((1, d), lambda bi, i: (0, 0))],
        out_specs=pl.BlockSpec((None, tm, d), lambda bi, i: (bi, i, 0)),
        out_shape=jax.ShapeDtypeStruct((b, t - n_ctx, d), F32),
        compiler_params=_params("arbitrary", "arbitrary"),
    )(x, g_final.reshape(1, d))


def _latent_permute(xs, n_ctx, to_col_major):
    b, t, d = xs.shape
    rows = (t - n_ctx) // GRID_W
    lat = xs[:, n_ctx:]
    if to_col_major:
        lat = lat.reshape(b, rows, GRID_W, d).transpose(0, 2, 1, 3)
    else:
        lat = lat.reshape(b, GRID_W, rows, d).transpose(0, 2, 1, 3)
    return jnp.concatenate([xs[:, :n_ctx], lat.reshape(b, t - n_ctx, d)], axis=1)


def _routing_plan(idx, rank, counts, n_tiles):
    n_e = counts.shape[-1]
    cnt = counts.reshape(n_e).astype(jnp.int32)
    tiles = (cnt + EXPERT_TILE - 1) // EXPERT_TILE
    tile_end = jnp.cumsum(tiles)
    start = (tile_end - tiles) * EXPERT_TILE
    onehot = idx[..., None] == jnp.arange(n_e, dtype=jnp.int32)
    dest = rank + jnp.sum(jnp.where(onehot, start, 0), axis=-1)
    tile_ids = jnp.arange(n_tiles, dtype=jnp.int32)
    tile_expert = jnp.sum(tile_ids[:, None] >= tile_end[None, :], axis=-1).astype(jnp.int32)
    tile_expert = jnp.minimum(tile_expert, n_e - 1)
    return dest.reshape(-1).astype(jnp.int32), tile_expert, tile_end[-1:].astype(jnp.int32)


def kernel(x, c, ctx, c_ctx, w_mod, b_mod, g_mix, g_ffn, w_in, conv_w, lru_w_a, lru_b_a, lru_w_i,
           lru_b_i, lru_lam, gdn_a_log, gdn_dt_bias, gdn_norm_g, w_out, w_router, router_bias,
           w_gate, w_up, w_down, ws_gate, ws_up, ws_down, g_final):
    b, seq, d = x.shape
    n_ctx = ctx.shape[1]
    t = n_ctx + seq
    depth = w_mod.shape[0]
    n_e = w_router.shape[-1]
    assert n_ctx % ROW_TILE == 0 and seq % ROW_TILE == 0 and b < 8
    n_tiles = (b * t * TOP_K) // EXPERT_TILE + n_e

    mods = _modulation(c, c_ctx, w_mod, b_mod)
    g_mix3 = g_mix.reshape(depth, 1, d)
    g_ffn3 = g_ffn.reshape(depth, 1, d)
    gn3 = gdn_norm_g.reshape(depth, 1, GDN_DV)
    rb3 = router_bias.reshape(depth, 1, n_e)
    w_out_bf = w_out.astype(BF16)
    ws_gate_bf, ws_up_bf, ws_down_bf = (w.astype(BF16) for w in (ws_gate, ws_up, ws_down))

    xs = jnp.concatenate([ctx, x], axis=1)
    for l in range(depth):
        col_major = l % 2 == 1
        if col_major:
            xs = _latent_permute(xs, n_ctx, True)
        h, pg = _inprep(xs, mods, g_mix3, w_in[l, :, MAIN_WIDTH:].astype(BF16), l, n_ctx)
        p = _inproj(h.reshape(b * t, d), w_in, l).reshape(b, t, MAIN_WIDTH)
        u = _conv(p, conv_w, l, n_ctx)
        y_lru = _lru(u, p, lru_w_a, lru_b_a, lru_w_i, lru_b_i, lru_lam, l, n_ctx)
        o_f, o_b = _gdn(u, pg, gdn_a_log, gdn_dt_bias, l, n_ctx)
        xs = _outproj(y_lru, o_f, o_b, p, xs, mods, gn3, w_out_bf, l, n_ctx)
        if col_major:
            xs = _latent_permute(xs, n_ctx, False)

        h2, idx, wts, rank, counts = _router(xs, mods, g_ffn3, w_router, rb3, l, n_ctx)
        dest, tile_expert, n_active = _routing_plan(idx, rank, counts, n_tiles)
        xsorted = _dispatch(h2, dest, n_tiles * EXPERT_TILE, d // LANE)
        ysorted = _experts(xsorted, tile_expert, n_active, w_gate, w_up, w_down, l)
        xs = _combine(ysorted, dest, wts.reshape(-1), xs, g_ffn3, mods, ws_gate_bf, ws_up_bf,
                      ws_down_bf, l, n_ctx)
    return _final_norm(xs, g_final, n_ctx)
```

```python
import functools

import jax
import jax.numpy as jnp
from jax import lax
from jax.experimental import pallas as pl
from jax.experimental.pallas import tpu as pltpu

F32 = jnp.float32
BF16 = jnp.bfloat16
HI = lax.Precision.HIGHEST

GRID_W = 64
N_MOD = 6
NORM_EPS = 1e-6
LRU_WIDTH = 1024
LRU_BLOCK = 128
LRU_C = 8.0
GDN_HEADS = 8
GDN_DK = 128
GDN_DV = 128
QK_WIDTH = GDN_HEADS * GDN_DK
V_WIDTH = GDN_HEADS * GDN_DV
CHUNK = 64
CONV_W = 4
CONV_PAD_LEFT = 2
CONV_CH = LRU_WIDTH + 2 * QK_WIDTH + V_WIDTH
MAIN_WIDTH = CONV_CH + LRU_WIDTH + V_WIDTH
N_GATE = 4 * GDN_HEADS
TOP_K = 8
ROUTED_SCALE = 2.5

LANE = 128
ROW_TILE = 256
VMEM_LIMIT = 56 * 1024 * 1024

NT_DIMS = (((1,), (1,)), ((), ()))
TN_DIMS = (((0,), (0,)), ((), ()))


def _params(*sem):
    return pltpu.CompilerParams(dimension_semantics=sem, vmem_limit_bytes=VMEM_LIMIT)


def _silu(x):
    return x * jax.nn.sigmoid(x)


def _norm_mod(x, g, shift, scale):
    ms = jnp.mean(x * x, axis=-1, keepdims=True)
    return (x * lax.rsqrt(ms + NORM_EPS) * g) * (1.0 + scale) + shift


def _to_blocks(y, blk_ref):
    tm, d = y.shape
    nb = d // LANE
    for s in range(nb):
        blk_ref[pl.ds(s, tm, stride=nb), :] = y[:, s * LANE:(s + 1) * LANE]


def _from_blocks(blk_ref, tm):
    nb = blk_ref.shape[0] // tm
    return jnp.concatenate([blk_ref[pl.ds(s, tm, stride=nb), :] for s in range(nb)], axis=-1)


def _mod_kernel(s_ref, w_ref, b_ref, o_ref):
    s = _silu(s_ref[...])
    o_ref[...] = jnp.dot(s.astype(BF16), w_ref[...].astype(BF16),
                         preferred_element_type=F32) + b_ref[...]


def _modulation(c, c_ctx, w_mod, b_mod):
    depth, d, nd = w_mod.shape
    b = c.shape[0]
    rows = jnp.concatenate([c, c_ctx[None], jnp.zeros((8 - b - 1, d), F32)], axis=0)
    tn = 1024
    out = pl.pallas_call(
        _mod_kernel,
        name="modulation",
        grid=(depth, nd // tn),
        in_specs=[pl.BlockSpec((8, d), lambda l, j: (0, 0)),
                  pl.BlockSpec((None, d, tn), lambda l, j: (l, 0, j)),
                  pl.BlockSpec((None, 1, tn), lambda l, j: (l, 0, j))],
        out_specs=pl.BlockSpec((None, 8, tn), lambda l, j: (l, 0, j)),
        out_shape=jax.ShapeDtypeStruct((depth, 8, nd), F32),
        compiler_params=_params("arbitrary", "arbitrary"),
    )(rows, w_mod, b_mod.reshape(depth, 1, nd))
    return out.reshape(depth, 8, N_MOD, d)


def _inprep_kernel(x_ref, mod_ref, g_ref, wg_ref, h_ref, pg_ref):
    h = _norm_mod(x_ref[...], g_ref[...], mod_ref[0:1, :], mod_ref[1:2, :])
    hb = h.astype(BF16)
    h_ref[...] = hb
    pg_ref[...] = jnp.dot(hb, wg_ref[...], preferred_element_type=F32)


def _inprep(x, mods, g, w_gatecols, l, n_ctx):
    b, t, d = x.shape
    tm = ROW_TILE
    nct = n_ctx // tm
    return pl.pallas_call(
        _inprep_kernel,
        name="inprep",
        grid=(b, t // tm),
        in_specs=[pl.BlockSpec((None, tm, d), lambda bi, i: (bi, i, 0)),
                  pl.BlockSpec((None, None, N_MOD, d),
                               lambda bi, i: (l, jnp.where(i < nct, b, bi), 0, 0)),
                  pl.BlockSpec((None, 1, d), lambda bi, i: (l, 0, 0)),
                  pl.BlockSpec((d, N_GATE), lambda bi, i: (0, 0))],
        out_specs=[pl.BlockSpec((None, tm, d), lambda bi, i: (bi, i, 0)),
                   pl.BlockSpec((None, tm, N_GATE), lambda bi, i: (bi, i, 0))],
        out_shape=[jax.ShapeDtypeStruct((b, t, d), BF16),
                   jax.ShapeDtypeStruct((b, t, N_GATE), F32)],
        compiler_params=_params("arbitrary", "arbitrary"),
    )(x, mods, g, w_gatecols)


def _matmul_kernel(h_ref, w_ref, o_ref, wb_ref):
    @pl.when(pl.program_id(1) == 0)
    def _():
        wb_ref[...] = w_ref[...].astype(BF16)

    o_ref[...] = jnp.dot(h_ref[...], wb_ref[...], preferred_element_type=F32)


def _inproj(h2d, w_in, l):
    n, d = h2d.shape
    tn, tm = 1024, 512
    return pl.pallas_call(
        _matmul_kernel,
        name="inproj",
        grid=(MAIN_WIDTH // tn, n // tm),
        in_specs=[pl.BlockSpec((tm, d), lambda j, i: (i, 0)),
                  pl.BlockSpec((None, d, tn), lambda j, i: (l, 0, j))],
        out_specs=pl.BlockSpec((tm, tn), lambda j, i: (i, j)),
        out_shape=jax.ShapeDtypeStruct((n, MAIN_WIDTH), F32),
        scratch_shapes=[pltpu.VMEM((d, tn), BF16)],
        compiler_params=_params("arbitrary", "arbitrary"),
    )(h2d, w_in)


def _conv_taps(x, cw_ref, t0, n_ctx, t_len, masked):
    rows = x.shape[0]
    if masked:
        t = t0 + lax.broadcasted_iota(jnp.int32, x.shape, 0)
        seg_t = jnp.where(t >= n_ctx, 1, 0)
    acc = jnp.zeros_like(x)
    for j in range(CONV_W):
        off = j - CONV_PAD_LEFT
        if off == 0:
            xs = x
        else:
            xs = pltpu.roll(x, (-off) % rows, 0)
            if masked:
                s = t + off
                seg_s = (jnp.where(s >= n_ctx, 1, 0) + jnp.where(s >= t_len, 1, 0)
                         - jnp.where(s < 0, 1, 0))
                xs = jnp.where(seg_s == seg_t, xs, 0.0)
        acc = acc + xs * cw_ref[j:j + 1, :]
    return acc


def _conv_kernel(p_ref, cw_ref, u_ref, acc_ref, *, n_ctx):
    kind = pl.program_id(1) // GDN_HEADS
    t_len = p_ref.shape[0]
    edge, win = 8, 24
    acc_ref[...] = _conv_taps(p_ref[...], cw_ref, 0, n_ctx, t_len, False)
    for r0 in (0, n_ctx - edge, n_ctx, t_len - edge):
        w0 = min(max(r0 - edge, 0), t_len - win)
        fixed = _conv_taps(p_ref[w0:w0 + win, :], cw_ref, w0, n_ctx, t_len, True)
        acc_ref[r0:r0 + edge, :] = fixed[r0 - w0:r0 - w0 + edge]
    acc = acc_ref[...]

    @pl.when(kind == 0)
    def _():
        u_ref[...] = acc

    @pl.when(kind == 3)
    def _():
        u_ref[...] = _silu(acc)

    @pl.when(jnp.logical_or(kind == 1, kind == 2))
    def _():
        y = _silu(acc)
        nrm = y * lax.rsqrt(jnp.sum(y * y, axis=-1, keepdims=True) + NORM_EPS)
        u_ref[...] = nrm * jnp.where(kind == 1, GDN_DK ** -0.5, 1.0)


def _conv(p, conv_w, l, n_ctx):
    b, t, _ = p.shape
    return pl.pallas_call(
        functools.partial(_conv_kernel, n_ctx=n_ctx),
        name="conv",
        grid=(b, CONV_CH // LANE),
        in_specs=[pl.BlockSpec((None, t, LANE), lambda bi, j: (bi, 0, j)),
                  pl.BlockSpec((None, CONV_W, LANE), lambda bi, j: (l, 0, j))],
        out_specs=pl.BlockSpec((None, t, LANE), lambda bi, j: (bi, 0, j)),
        out_shape=jax.ShapeDtypeStruct((b, t, CONV_CH), F32),
        scratch_shapes=[pltpu.VMEM((t, LANE), F32)],
        compiler_params=_params("arbitrary", "arbitrary"),
    )(p, conv_w)


LRU_LANES = 512


def _lru_kernel(u_ref, y_ref, wa_ref, ba_ref, wi_ref, bi_ref, lam_ref, o_ref,
                af_ref, bf_ref, ab_ref, bb_ref, *, n_ctx):
    t_len, width = u_ref.shape
    nblk = width // LRU_BLOCK
    tm = ROW_TILE

    def gates(i, carry):
        r0 = pl.multiple_of(i * tm, tm)
        u = u_ref[pl.ds(r0, tm), :]
        ub = u.astype(BF16)
        for d, (a_ref, b_ref) in enumerate(((af_ref, bf_ref), (ab_ref, bb_ref))):
            pre_a = jnp.concatenate(
                [jnp.dot(ub[:, n * LRU_BLOCK:(n + 1) * LRU_BLOCK], wa_ref[d, n].astype(BF16),
                         preferred_element_type=F32) for n in range(nblk)], axis=-1)
            pre_i = jnp.concatenate(
                [jnp.dot(ub[:, n * LRU_BLOCK:(n + 1) * LRU_BLOCK], wi_ref[d, n].astype(BF16),
                         preferred_element_type=F32) for n in range(nblk)], axis=-1)
            r = jax.nn.sigmoid(pre_a + ba_ref[d:d + 1, :])
            gi = jax.nn.sigmoid(pre_i + bi_ref[d:d + 1, :])
            lam = lam_ref[d:d + 1, :]
            log_sig = jnp.minimum(lam, 0.0) - jnp.log(1.0 + jnp.exp(-jnp.abs(lam)))
            log_a = LRU_C * r * log_sig
            a = jnp.exp(log_a)
            bcoef = jnp.sqrt(jnp.maximum(1.0 - jnp.exp(2.0 * log_a), 0.0)) * (gi * u)
            a_ref[pl.ds(r0, tm), :] = a
            b_ref[pl.ds(r0, tm), :] = bcoef
        return carry

    lax.fori_loop(0, t_len // tm, gates, 0)

    def scan(gi, carry):
        hf, hb = carry
        base = pl.multiple_of(gi * 8, 8)
        bbase = jnp.where(base < n_ctx, n_ctx - 1 - base, t_len + n_ctx - 1 - base)
        for j in range(8):
            tf = base + j
            tb = bbase - j
            hf = af_ref[pl.ds(tf, 1), :] * hf + bf_ref[pl.ds(tf, 1), :]
            af_ref[pl.ds(tf, 1), :] = hf
            hb = ab_ref[pl.ds(tb, 1), :] * hb + bb_ref[pl.ds(tb, 1), :]
            ab_ref[pl.ds(tb, 1), :] = hb
        return hf, hb

    zero = jnp.zeros((1, width), F32)
    lax.fori_loop(0, t_len // 8, scan, (zero, zero))
    o_ref[...] = jax.nn.gelu(y_ref[...], approximate=True) * (af_ref[...] + ab_ref[...])


def _lru(u, p, lru_w_a, lru_b_a, lru_w_i, lru_b_i, lru_lam, l, n_ctx):
    b, t, _ = u.shape
    w = LRU_LANES
    nb = w // LRU_BLOCK
    y_off = CONV_CH // w
    wspec = pl.BlockSpec((None, 2, nb, LRU_BLOCK, LRU_BLOCK), lambda bi, j: (l, 0, j, 0, 0))
    vspec = pl.BlockSpec((None, 2, w), lambda bi, j: (l, 0, j))
    return pl.pallas_call(
        functools.partial(_lru_kernel, n_ctx=n_ctx),
        name="lru",
        grid=(b, LRU_WIDTH // w),
        in_specs=[pl.BlockSpec((None, t, w), lambda bi, j: (bi, 0, j)),
                  pl.BlockSpec((None, t, w), lambda bi, j: (bi, 0, y_off + j)),
                  wspec, vspec, wspec, vspec, vspec],
        out_specs=pl.BlockSpec((None, t, w), lambda bi, j: (bi, 0, j)),
        out_shape=jax.ShapeDtypeStruct((b, t, LRU_WIDTH), F32),
        scratch_shapes=[pltpu.VMEM((t, w), F32)] * 4,
        compiler_params=_params("arbitrary", "arbitrary"),
    )(u, p, lru_w_a, lru_b_a, lru_w_i, lru_b_i, lru_lam)


def _split(a):
    hi = a.astype(BF16).astype(F32)
    return hi, a - hi


def _lhs3(a):
    hi, lo = _split(a)
    return jnp.concatenate([hi, lo, hi], axis=1).astype(BF16)


def _rhs3(b):
    hi, lo = _split(b)
    return jnp.concatenate([hi, hi, lo], axis=0).astype(BF16)


def _gdn_kernel(qf_ref, kf_ref, vf_ref, gf_ref, qb_ref, kb_ref, vb_ref, gb_ref, alog_ref, dtb_ref,
                of_ref, ob_ref, s_ref):
    c, h_n, dk, dv = CHUNK, GDN_HEADS, GDN_DK, GDN_DV

    @pl.when(pl.program_id(1) == 0)
    def _():
        s_ref[...] = jnp.zeros_like(s_ref)

    ri = lax.broadcasted_iota(jnp.int32, (c, c), 0)
    ci = lax.broadcasted_iota(jnp.int32, (c, c), 1)
    dirs = ((qf_ref, kf_ref, vf_ref, gf_ref, of_ref), (qb_ref, kb_ref, vb_ref, gb_ref, ob_ref))

    chains = []
    for d, (q_ref, k_ref, v_ref, gt_ref, o_ref) in enumerate(dirs):
        gt = gt_ref[...]
        beta = jax.nn.sigmoid(gt[:, d * h_n:(d + 1) * h_n])
        xa = gt[:, (2 + d) * h_n:(3 + d) * h_n] + dtb_ref[d:d + 1, :]
        softplus = jnp.maximum(xa, 0.0) + jnp.log(1.0 + jnp.exp(-jnp.abs(xa)))
        g = -jnp.exp(alog_ref[d:d + 1, :]) * softplus
        incl = (ri <= ci) if d else (ri >= ci)
        strict = (ri < ci) if d else (ri > ci)
        ones_incl = jnp.where(incl, 1.0, 0.0)
        cum = jnp.dot(ones_incl, g, precision=HI, preferred_element_type=F32)
        cum_t = lax.dot_general(g, ones_incl, (((0,), (1,)), ((), ())), precision=HI,
                                preferred_element_type=F32)
        total = cum[0:1, :] if d else cum[c - 1:c, :]
        e_cum = jnp.exp(cum)
        e_rest = jnp.exp(total - cum)
        e_total = jnp.exp(total)
        for h in range(h_n):
            sl = slice(h * dk, (h + 1) * dk)
            k = k_ref[:, sl]
            kb = k.astype(BF16)
            diff = jnp.where(incl, cum[:, h:h + 1] - cum_t[h:h + 1, :], 0.0)
            gamma = jnp.where(incl, jnp.exp(diff), 0.0)
            bcol = beta[:, h:h + 1]
            kk = lax.dot_general(kb, kb, NT_DIMS, preferred_element_type=F32)
            chains.append(dict(
                d=d, h=h, q_ref=q_ref, k_ref=k_ref, o_ref=o_ref, gamma=gamma,
                neg=jnp.where(strict, -(bcol * kk * gamma), 0.0),
                x=jnp.concatenate([bcol * v_ref[:, sl], bcol * k * e_cum[:, h:h + 1]], axis=-1),
                e_cum=e_cum[:, h:h + 1], e_rest=e_rest[:, h:h + 1], e_total=e_total[:, h:h + 1]))

    for level in range(6):
        for ch in chains:
            a3 = _lhs3(ch["neg"])
            ch["x"] = ch["x"] + jnp.dot(a3, _rhs3(ch["x"]), preferred_element_type=F32)
            if level < 5:
                ch["neg"] = jnp.dot(a3, _rhs3(ch["neg"]), preferred_element_type=F32)

    for ch in chains:
        d, h = ch["d"], ch["h"]
        sl = slice(h * dk, (h + 1) * dk)
        q = ch["q_ref"][:, sl]
        k = ch["k_ref"][:, sl]
        u = ch["x"][:, :dv]
        w = ch["x"][:, dv:]
        qk = lax.dot_general(q.astype(BF16), k.astype(BF16), NT_DIMS,
                             preferred_element_type=F32) * ch["gamma"]
        q_dec = q * ch["e_cum"]
        k_dec = k * ch["e_rest"]
        s = s_ref[d, h]
        ws = jnp.dot(jnp.concatenate([w, q_dec], axis=0).astype(BF16), s.astype(BF16),
                     preferred_element_type=F32)
        v_new = u - ws[:c]
        vb = v_new.astype(BF16)
        ch["o_ref"][:, h * dv:(h + 1) * dv] = ws[c:] + jnp.dot(qk.astype(BF16), vb,
                                                               preferred_element_type=F32)
        s_ref[d, h] = ch["e_total"] * s + lax.dot_general(
            k_dec.astype(BF16), vb, TN_DIMS, preferred_element_type=F32)


def _gdn(u, pg, gdn_a_log, gdn_dt_bias, l, n_ctx):
    b, t, _ = u.shape
    nc = t // CHUNK
    ncc = n_ctx // CHUNK

    def rev(s):
        return jnp.where(s < ncc, ncc - 1 - s, nc + ncc - 1 - s)

    def specs(chunk):
        qkv = [pl.BlockSpec((None, CHUNK, QK_WIDTH), lambda bi, s, col=col: (bi, chunk(s), col))
               for col in (1, 2, 3)]
        return qkv + [pl.BlockSpec((None, CHUNK, N_GATE), lambda bi, s: (bi, chunk(s), 0))]

    fwd = lambda s: s
    small = pl.BlockSpec((None, 2, GDN_HEADS), lambda bi, s: (l, 0, 0))
    out = jax.ShapeDtypeStruct((b, t, V_WIDTH), F32)
    return pl.pallas_call(
        _gdn_kernel,
        name="gdn",
        grid=(b, nc),
        in_specs=specs(fwd) + specs(rev) + [small, small],
        out_specs=[pl.BlockSpec((None, CHUNK, V_WIDTH), lambda bi, s: (bi, s, 0)),
                   pl.BlockSpec((None, CHUNK, V_WIDTH), lambda bi, s: (bi, rev(s), 0))],
        out_shape=[out, out],
        scratch_shapes=[pltpu.VMEM((2, GDN_HEADS, GDN_DK, GDN_DV), F32)],
        compiler_params=_params("arbitrary", "arbitrary"),
    )(u, u, u, pg, u, u, u, pg, gdn_a_log, gdn_dt_bias)


def _outproj_kernel(ylru_ref, of_ref, ob_ref, z_ref, x_ref, mod_ref, gn_ref, w_ref, o_ref):
    o = of_ref[...] + ob_ref[...]
    z = z_ref[...]
    parts = [ylru_ref[...]]
    for h in range(GDN_HEADS):
        sl = slice(h * GDN_DV, (h + 1) * GDN_DV)
        oh = o[:, sl]
        ms = jnp.mean(oh * oh, axis=-1, keepdims=True)
        parts.append(oh * lax.rsqrt(ms + NORM_EPS) * gn_ref[...] * _silu(z[:, sl]))
    feats = jnp.concatenate(parts, axis=-1).astype(BF16)
    y = jnp.dot(feats, w_ref[...], preferred_element_type=F32)
    o_ref[...] = x_ref[...] + mod_ref[2:3, :] * y


def _outproj(y_lru, o_f, o_b, p, x, mods, gdn_norm_g, w_out_bf, l, n_ctx):
    b, t, d = x.shape
    tm = ROW_TILE
    nct = n_ctx // tm
    z_off = (CONV_CH + LRU_WIDTH) // V_WIDTH
    row = lambda bi, i: (bi, i, 0)
    return pl.pallas_call(
        _outproj_kernel,
        name="outproj",
        grid=(b, t // tm),
        in_specs=[pl.BlockSpec((None, tm, LRU_WIDTH), row),
                  pl.BlockSpec((None, tm, V_WIDTH), row),
                  pl.BlockSpec((None, tm, V_WIDTH), row),
                  pl.BlockSpec((None, tm, V_WIDTH), lambda bi, i: (bi, i, z_off)),
                  pl.BlockSpec((None, tm, d), row),
                  pl.BlockSpec((None, None, N_MOD, d),
                               lambda bi, i: (l, jnp.where(i < nct, b, bi), 0, 0)),
                  pl.BlockSpec((None, 1, GDN_DV), lambda bi, i: (l, 0, 0)),
                  pl.BlockSpec((None, LRU_WIDTH + V_WIDTH, d), lambda bi, i: (l, 0, 0))],
        out_specs=pl.BlockSpec((None, tm, d), row),
        out_shape=jax.ShapeDtypeStruct((b, t, d), F32),
        compiler_params=_params("arbitrary", "arbitrary"),
    )(y_lru, o_f, o_b, p, x, mods, gdn_norm_g, w_out_bf)


def _router_kernel(x_ref, mod_ref, g_ref, wr_ref, rb_ref, h_ref, idx_ref, wt_ref, rank_ref,
                   cnt_ref, run_ref, blk_ref):
    first = jnp.logical_and(pl.program_id(0) == 0, pl.program_id(1) == 0)

    @pl.when(first)
    def _():
        run_ref[...] = jnp.zeros_like(run_ref)

    h = _norm_mod(x_ref[...], g_ref[...], mod_ref[3:4, :], mod_ref[4:5, :])
    scores = jax.nn.sigmoid(jnp.dot(h, wr_ref[...], precision=HI, preferred_element_type=F32))
    tm, n_e = scores.shape
    _to_blocks(h, blk_ref)
    h_ref[...] = blk_ref[...].astype(BF16)
    lane = lax.broadcasted_iota(jnp.int32, (tm, n_e), 1)
    kl = lax.broadcasted_iota(jnp.int32, (tm, TOP_K), 1)
    biased = scores + rb_ref[...]
    idx = jnp.zeros((tm, TOP_K), jnp.int32)
    sel = jnp.zeros((tm, TOP_K), F32)
    maskf = jnp.zeros((tm, n_e), F32)
    onehots = []
    for k in range(TOP_K):
        m = jnp.max(biased, axis=-1, keepdims=True)
        ik = jnp.min(jnp.where(biased == m, lane, n_e), axis=-1, keepdims=True)
        oh = lane == ik
        onehots.append(oh)
        sk = jnp.sum(jnp.where(oh, scores, 0.0), axis=-1, keepdims=True)
        idx = jnp.where(kl == k, ik, idx)
        sel = jnp.where(kl == k, sk, sel)
        maskf = jnp.where(oh, 1.0, maskf)
        biased = jnp.where(oh, -jnp.inf, biased)
    wt_ref[...] = sel / jnp.sum(sel, axis=-1, keepdims=True) * ROUTED_SCALE
    idx_ref[...] = idx

    ri = lax.broadcasted_iota(jnp.int32, (tm, tm), 0)
    ci = lax.broadcasted_iota(jnp.int32, (tm, tm), 1)
    before = jnp.where(ri > ci, 1.0, 0.0).astype(BF16)
    prior = jnp.dot(before, maskf.astype(BF16), preferred_element_type=F32) + run_ref[...]
    rank = jnp.zeros((tm, TOP_K), F32)
    for k in range(TOP_K):
        rk = jnp.sum(jnp.where(onehots[k], prior, 0.0), axis=-1, keepdims=True)
        rank = jnp.where(kl == k, rk, rank)
    rank_ref[...] = rank.astype(jnp.int32)
    run_ref[...] = run_ref[...] + jnp.sum(maskf, axis=0, keepdims=True)
    cnt_ref[...] = run_ref[...]


def _router(x, mods, g, w_router, router_bias, l, n_ctx):
    b, t, d = x.shape
    n_e = w_router.shape[-1]
    tm = ROW_TILE
    nt = t // tm
    nct = n_ctx // tm
    nb = d // LANE
    row = lambda bi, i: (bi, i, 0)
    return pl.pallas_call(
        _router_kernel,
        name="router",
        grid=(b, nt),
        in_specs=[pl.BlockSpec((None, tm, d), row),
                  pl.BlockSpec((None, None, N_MOD, d),
                               lambda bi, i: (l, jnp.where(i < nct, b, bi), 0, 0)),
                  pl.BlockSpec((None, 1, d), lambda bi, i: (l, 0, 0)),
                  pl.BlockSpec((None, d, n_e), lambda bi, i: (l, 0, 0)),
                  pl.BlockSpec((None, 1, n_e), lambda bi, i: (l, 0, 0))],
        out_specs=[pl.BlockSpec((tm * nb, LANE), lambda bi, i: (bi * nt + i, 0)),
                   pl.BlockSpec((None, tm, TOP_K), row),
                   pl.BlockSpec((None, tm, TOP_K), row),
                   pl.BlockSpec((None, tm, TOP_K), row),
                   pl.BlockSpec((1, n_e), lambda bi, i: (0, 0))],
        out_shape=[jax.ShapeDtypeStruct((b * t * nb, LANE), BF16),
                   jax.ShapeDtypeStruct((b, t, TOP_K), jnp.int32),
                   jax.ShapeDtypeStruct((b, t, TOP_K), F32),
                   jax.ShapeDtypeStruct((b, t, TOP_K), jnp.int32),
                   jax.ShapeDtypeStruct((1, n_e), F32)],
        scratch_shapes=[pltpu.VMEM((1, n_e), F32), pltpu.VMEM((tm * nb, LANE), F32)],
        compiler_params=_params("arbitrary", "arbitrary"),
    )(x, mods, g, w_router, router_bias)


DISPATCH_TOKENS = 256


def _dispatch_kernel(dest_ref, h_ref, xs_in_ref, xs_ref, sem, *, nb):
    del xs_in_ref

    def issue(r, carry):
        src = h_ref.at[pl.ds(pl.multiple_of(r * nb, nb), nb)]
        for k in range(TOP_K):
            slot = dest_ref[r * TOP_K + k]
            pltpu.make_async_copy(src, xs_ref.at[pl.ds(pl.multiple_of(slot * nb, nb), nb)],
                                  sem).start()
        return carry

    lax.fori_loop(0, DISPATCH_TOKENS, issue, 0)
    for k in range(TOP_K):
        pltpu.make_async_copy(h_ref, xs_ref.at[pl.ds(0, DISPATCH_TOKENS * nb)], sem).wait()


def _dispatch(h_blocks, dest_flat, n_slots, nb):
    n = h_blocks.shape[0] // nb
    xs0 = jnp.zeros((n_slots * nb, LANE), h_blocks.dtype)
    return pl.pallas_call(
        functools.partial(_dispatch_kernel, nb=nb),
        name="dispatch",
        grid=(n // DISPATCH_TOKENS,),
        in_specs=[pl.BlockSpec((DISPATCH_TOKENS * TOP_K,), lambda i: (i,),
                               memory_space=pltpu.SMEM),
                  pl.BlockSpec((DISPATCH_TOKENS * nb, LANE), lambda i: (i, 0)),
                  pl.BlockSpec(memory_space=pl.ANY)],
        out_specs=pl.BlockSpec(memory_space=pl.ANY),
        out_shape=jax.ShapeDtypeStruct((n_slots * nb, LANE), h_blocks.dtype),
        scratch_shapes=[pltpu.SemaphoreType.DMA(())],
        input_output_aliases={2: 0},
        compiler_params=_params("arbitrary"),
    )(dest_flat, h_blocks, xs0)


EXPERT_TILE = 256


def _expert_kernel(te_ref, na_ref, x_ref, wg_ref, wu_ref, wd_ref, y_ref, wgb_ref, wub_ref, wdb_ref,
                   blk_ref):
    i = pl.program_id(0)
    active = i < na_ref[0]
    changed = jnp.logical_or(i == 0, te_ref[i] != te_ref[jnp.maximum(i - 1, 0)])

    @pl.when(jnp.logical_and(active, changed))
    def _():
        wgb_ref[...] = wg_ref[...].astype(BF16)
        wub_ref[...] = wu_ref[...].astype(BF16)
        wdb_ref[...] = wd_ref[...].astype(BF16)

    @pl.when(active)
    def _():
        blk_ref[...] = x_ref[...].astype(F32)
        xb = _from_blocks(blk_ref, EXPERT_TILE).astype(BF16)
        hid = _silu(jnp.dot(xb, wgb_ref[...], preferred_element_type=F32)) * jnp.dot(
            xb, wub_ref[...], preferred_element_type=F32)
        y_ref[...] = jnp.dot(hid.astype(BF16), wdb_ref[...], preferred_element_type=F32)

    @pl.when(jnp.logical_not(active))
    def _():
        y_ref[...] = jnp.zeros_like(y_ref)


def _experts(xs, tile_expert, n_active, w_gate, w_up, w_down, l):
    d, f = w_gate.shape[-2:]
    nb = d // LANE
    n_slots = xs.shape[0] // nb
    tm = EXPERT_TILE
    grid_spec = pltpu.PrefetchScalarGridSpec(
        num_scalar_prefetch=2,
        grid=(n_slots // tm,),
        in_specs=[pl.BlockSpec((tm * nb, LANE), lambda i, te, na: (i, 0)),
                  pl.BlockSpec((None, None, d, f), lambda i, te, na: (l, te[i], 0, 0)),
                  pl.BlockSpec((None, None, d, f), lambda i, te, na: (l, te[i], 0, 0)),
                  pl.BlockSpec((None, None, f, d), lambda i, te, na: (l, te[i], 0, 0))],
        out_specs=pl.BlockSpec((tm, d), lambda i, te, na: (i, 0)),
        scratch_shapes=[pltpu.VMEM((d, f), BF16), pltpu.VMEM((d, f), BF16),
                        pltpu.VMEM((f, d), BF16), pltpu.VMEM((tm * nb, LANE), F32)],
    )
    return pl.pallas_call(
        _expert_kernel,
        name="experts",
        grid_spec=grid_spec,
        out_shape=jax.ShapeDtypeStruct((n_slots, d), F32),
        compiler_params=_params("arbitrary"),
    )(tile_expert, n_active, xs, w_gate, w_up, w_down)


COMBINE_TOKENS = 128


def _combine_kernel(dest_ref, ys_ref, x_ref, g_ref, wt_ref, mod_ref, sg_ref, su_ref, sd_ref,
                    o_ref, buf_ref, sem):
    tm = x_ref.shape[0]

    def issue(r, carry):
        for k in range(TOP_K):
            slot = dest_ref[r * TOP_K + k]
            pltpu.make_async_copy(ys_ref.at[pl.ds(slot, 1)], buf_ref.at[pl.ds(k * tm + r, 1)],
                                  sem).start()
        return carry

    lax.fori_loop(0, tm, issue, 0)
    x = x_ref[...]
    hb = _norm_mod(x, g_ref[...], mod_ref[3:4, :], mod_ref[4:5, :]).astype(BF16)
    hid = _silu(jnp.dot(hb, sg_ref[...], preferred_element_type=F32)) * jnp.dot(
        hb, su_ref[...], preferred_element_type=F32)
    y = jnp.dot(hid.astype(BF16), sd_ref[...], preferred_element_type=F32)
    pltpu.make_async_copy(ys_ref.at[pl.ds(0, TOP_K * tm)], buf_ref, sem).wait()
    wt = wt_ref[...]
    for k in range(TOP_K):
        y = y + wt[:, k:k + 1] * buf_ref[pl.ds(k * tm, tm), :]
    o_ref[...] = x + mod_ref[5:6, :] * y


def _combine(ys, dest_flat, wts, x, g, mods, ws_gate_bf, ws_up_bf, ws_down_bf, l, n_ctx):
    b, t, d = x.shape
    tm = COMBINE_TOKENS
    nt = t // tm
    nct = n_ctx // tm
    fs = ws_gate_bf.shape[-1]
    row = lambda bi, i: (bi, i, 0)
    return pl.pallas_call(
        _combine_kernel,
        name="combine",
        grid=(b, nt),
        in_specs=[pl.BlockSpec((tm * TOP_K,), lambda bi, i: (bi * nt + i,),
                               memory_space=pltpu.SMEM),
                  pl.BlockSpec(memory_space=pl.ANY),
                  pl.BlockSpec((None, tm, d), row),
                  pl.BlockSpec((None, 1, d), lambda bi, i: (l, 0, 0)),
                  pl.BlockSpec((None, tm, TOP_K), row),
                  pl.BlockSpec((None, None, N_MOD, d),
                               lambda bi, i: (l, jnp.where(i < nct, b, bi), 0, 0)),
                  pl.BlockSpec((None, d, fs), lambda bi, i: (l, 0, 0)),
                  pl.BlockSpec((None, d, fs), lambda bi, i: (l, 0, 0)),
                  pl.BlockSpec((None, fs, d), lambda bi, i: (l, 0, 0))],
        out_specs=pl.BlockSpec((None, tm, d), row),
        out_shape=jax.ShapeDtypeStruct((b, t, d), F32),
        scratch_shapes=[pltpu.VMEM((TOP_K * tm, d), F32), pltpu.SemaphoreType.DMA(())],
        compiler_params=_params("arbitrary", "arbitrary"),
    )(dest_flat, ys, x, g, wts, mods, ws_gate_bf, ws_up_bf, ws_down_bf)


def _final_kernel(x_ref, g_ref, o_ref):
    x = x_ref[...]
    ms = jnp.mean(x * x, axis=-1, keepdims=True)
    o_ref[...] = x * lax.rsqrt(ms + NORM_EPS) * g_ref[...]


def _final_norm(x, g_final, n_ctx):
    b, t, d = x.shape
    tm = ROW_TILE
    nct = n_ctx // tm
    return pl.pallas_call(
        _final_kernel,
        name="final_norm",
        grid=(b, (t - n_ctx) // tm),
        in_specs=[pl.BlockSpec((None, tm, d), lambda bi, i: (bi, i + nct, 0)),
                  pl.BlockSpec((1, d), lambda bi, i: (0, 0))],
        out_specs=pl.BlockSpec((None, tm, d), lambda bi, i: (bi, i, 0)),
        out_shape=jax.ShapeDtypeStruct((b, t - n_ctx, d), F32),
        compiler_params=_params("arbitrary", "arbitrary"),
    )(x, g_final.reshape(1, d))


def _latent_permute(xs, n_ctx, to_col_major):
    b, t, d = xs.shape
    rows = (t - n_ctx) // GRID_W
    lat = xs[:, n_ctx:]
    if to_col_major:
        lat = lat.reshape(b, rows, GRID_W, d).transpose(0, 2, 1, 3)
    else:
        lat = lat.reshape(b, GRID_W, rows, d).transpose(0, 2, 1, 3)
    return jnp.concatenate([xs[:, :n_ctx], lat.reshape(b, t - n_ctx, d)], axis=1)


def _routing_plan(idx, rank, counts, n_tiles):
    n_e = counts.shape[-1]
    cnt = counts.reshape(n_e).astype(jnp.int32)
    tiles = (cnt + EXPERT_TILE - 1) // EXPERT_TILE
    tile_end = jnp.cumsum(tiles)
    start = (tile_end - tiles) * EXPERT_TILE
    onehot = idx[..., None] == jnp.arange(n_e, dtype=jnp.int32)
    dest = rank + jnp.sum(jnp.where(onehot, start, 0), axis=-1)
    tile_ids = jnp.arange(n_tiles, dtype=jnp.int32)
    tile_expert = jnp.sum(tile_ids[:, None] >= tile_end[None, :], axis=-1).astype(jnp.int32)
    tile_expert = jnp.minimum(tile_expert, n_e - 1)
    return dest.reshape(-1).astype(jnp.int32), tile_expert, tile_end[-1:].astype(jnp.int32)


def kernel(x, c, ctx, c_ctx, w_mod, b_mod, g_mix, g_ffn, w_in, conv_w, lru_w_a, lru_b_a, lru_w_i,
           lru_b_i, lru_lam, gdn_a_log, gdn_dt_bias, gdn_norm_g, w_out, w_router, router_bias,
           w_gate, w_up, w_down, ws_gate, ws_up, ws_down, g_final):
    b, seq, d = x.shape
    n_ctx = ctx.shape[1]
    t = n_ctx + seq
    depth = w_mod.shape[0]
    n_e = w_router.shape[-1]
    assert n_ctx % ROW_TILE == 0 and seq % ROW_TILE == 0 and b < 8
    n_tiles = (b * t * TOP_K) // EXPERT_TILE + n_e

    mods = _modulation(c, c_ctx, w_mod, b_mod)
    g_mix3 = g_mix.reshape(depth, 1, d)
    g_ffn3 = g_ffn.reshape(depth, 1, d)
    gn3 = gdn_norm_g.reshape(depth, 1, GDN_DV)
    rb3 = router_bias.reshape(depth, 1, n_e)
    w_out_bf = w_out.astype(BF16)
    ws_gate_bf, ws_up_bf, ws_down_bf = (w.astype(BF16) for w in (ws_gate, ws_up, ws_down))

    xs = jnp.concatenate([ctx, x], axis=1)
    for l in range(depth):
        col_major = l % 2 == 1
        if col_major:
            xs = _latent_permute(xs, n_ctx, True)
        h, pg = _inprep(xs, mods, g_mix3, w_in[l, :, MAIN_WIDTH:].astype(BF16), l, n_ctx)
        p = _inproj(h.reshape(b * t, d), w_in, l).reshape(b, t, MAIN_WIDTH)
        u = _conv(p, conv_w, l, n_ctx)
        y_lru = _lru(u, p, lru_w_a, lru_b_a, lru_w_i, lru_b_i, lru_lam, l, n_ctx)
        o_f, o_b = _gdn(u, pg, gdn_a_log, gdn_dt_bias, l, n_ctx)
        xs = _outproj(y_lru, o_f, o_b, p, xs, mods, gn3, w_out_bf, l, n_ctx)
        if col_major:
            xs = _latent_permute(xs, n_ctx, False)

        h2, idx, wts, rank, counts = _router(xs, mods, g_ffn3, w_router, rb3, l, n_ctx)
        dest, tile_expert, n_active = _routing_plan(idx, rank, counts, n_tiles)
        xsorted = _dispatch(h2, dest, n_tiles * EXPERT_TILE, d // LANE)
        ysorted = _experts(xsorted, tile_expert, n_active, w_gate, w_up, w_down, l)
        xs = _combine(ysorted, dest, wts, xs, g_ffn3, mods, ws_gate_bf, ws_up_bf, ws_down_bf, l,
                      n_ctx)
    return _final_norm(xs, g_final, n_ctx)
```

```python
import functools

import jax
import jax.numpy as jnp
from jax import lax
from jax.experimental import pallas as pl
from jax.experimental.pallas import tpu as pltpu

F32 = jnp.float32
BF16 = jnp.bfloat16
HI = lax.Precision.HIGHEST

GRID_W = 64
N_MOD = 6
NORM_EPS = 1e-6
LRU_WIDTH = 1024
LRU_BLOCK = 128
LRU_C = 8.0
GDN_HEADS = 8
GDN_DK = 128
GDN_DV = 128
QK_WIDTH = GDN_HEADS * GDN_DK
V_WIDTH = GDN_HEADS * GDN_DV
CHUNK = 64
CONV_W = 4
CONV_PAD_LEFT = 2
CONV_CH = LRU_WIDTH + 2 * QK_WIDTH + V_WIDTH
MAIN_WIDTH = CONV_CH + LRU_WIDTH + V_WIDTH
N_GATE = 4 * GDN_HEADS
TOP_K = 8
ROUTED_SCALE = 2.5

LANE = 128
ROW_TILE = 256
VMEM_LIMIT = 56 * 1024 * 1024

NT_DIMS = (((1,), (1,)), ((), ()))
TN_DIMS = (((0,), (0,)), ((), ()))


def _params(*sem):
    return pltpu.CompilerParams(dimension_semantics=sem, vmem_limit_bytes=VMEM_LIMIT)


def _silu(x):
    return x * jax.nn.sigmoid(x)


def _norm_mod(x, g, shift, scale):
    ms = jnp.mean(x * x, axis=-1, keepdims=True)
    return (x * lax.rsqrt(ms + NORM_EPS) * g) * (1.0 + scale) + shift


def _to_blocks(y, blk_ref):
    tm, d = y.shape
    nb = d // LANE
    for s in range(nb):
        blk_ref[pl.ds(s, tm, stride=nb), :] = y[:, s * LANE:(s + 1) * LANE]


def _from_blocks(blk_ref, tm):
    nb = blk_ref.shape[0] // tm
    return jnp.concatenate([blk_ref[pl.ds(s, tm, stride=nb), :] for s in range(nb)], axis=-1)


def _mod_kernel(s_ref, w_ref, b_ref, o_ref):
    s = _silu(s_ref[...])
    o_ref[...] = jnp.dot(s.astype(BF16), w_ref[...].astype(BF16),
                         preferred_element_type=F32) + b_ref[...]


def _modulation(c, c_ctx, w_mod, b_mod):
    depth, d, nd = w_mod.shape
    b = c.shape[0]
    rows = jnp.concatenate([c, c_ctx[None], jnp.zeros((8 - b - 1, d), F32)], axis=0)
    tn = 1024
    out = pl.pallas_call(
        _mod_kernel,
        name="modulation",
        grid=(depth, nd // tn),
        in_specs=[pl.BlockSpec((8, d), lambda l, j: (0, 0)),
                  pl.BlockSpec((None, d, tn), lambda l, j: (l, 0, j)),
                  pl.BlockSpec((None, 1, tn), lambda l, j: (l, 0, j))],
        out_specs=pl.BlockSpec((None, 8, tn), lambda l, j: (l, 0, j)),
        out_shape=jax.ShapeDtypeStruct((depth, 8, nd), F32),
        compiler_params=_params("arbitrary", "arbitrary"),
    )(rows, w_mod, b_mod.reshape(depth, 1, nd))
    return out.reshape(depth, 8, N_MOD, d)


def _inprep_kernel(x_ref, mod_ref, g_ref, wg_ref, h_ref, pg_ref):
    h = _norm_mod(x_ref[...], g_ref[...], mod_ref[0:1, :], mod_ref[1:2, :])
    hb = h.astype(BF16)
    h_ref[...] = hb
    pg_ref[...] = jnp.dot(hb, wg_ref[...], preferred_element_type=F32)


def _inprep(x, mods, g, w_gatecols, l, n_ctx):
    b, t, d = x.shape
    tm = ROW_TILE
    nct = n_ctx // tm
    return pl.pallas_call(
        _inprep_kernel,
        name="inprep",
        grid=(b, t // tm),
        in_specs=[pl.BlockSpec((None, tm, d), lambda bi, i: (bi, i, 0)),
                  pl.BlockSpec((None, None, N_MOD, d),
                               lambda bi, i: (l, jnp.where(i < nct, b, bi), 0, 0)),
                  pl.BlockSpec((None, 1, d), lambda bi, i: (l, 0, 0)),
                  pl.BlockSpec((d, N_GATE), lambda bi, i: (0, 0))],
        out_specs=[pl.BlockSpec((None, tm, d), lambda bi, i: (bi, i, 0)),
                   pl.BlockSpec((None, tm, N_GATE), lambda bi, i: (bi, i, 0))],
        out_shape=[jax.ShapeDtypeStruct((b, t, d), BF16),
                   jax.ShapeDtypeStruct((b, t, N_GATE), F32)],
        compiler_params=_params("arbitrary", "arbitrary"),
    )(x, mods, g, w_gatecols)


def _matmul_kernel(h_ref, w_ref, o_ref, wb_ref):
    @pl.when(pl.program_id(1) == 0)
    def _():
        wb_ref[...] = w_ref[...].astype(BF16)

    o_ref[...] = jnp.dot(h_ref[...], wb_ref[...], preferred_element_type=F32)


def _inproj(h2d, w_in, l):
    n, d = h2d.shape
    tn, tm = 1024, 512
    return pl.pallas_call(
        _matmul_kernel,
        name="inproj",
        grid=(MAIN_WIDTH // tn, n // tm),
        in_specs=[pl.BlockSpec((tm, d), lambda j, i: (i, 0)),
                  pl.BlockSpec((None, d, tn), lambda j, i: (l, 0, j))],
        out_specs=pl.BlockSpec((tm, tn), lambda j, i: (i, j)),
        out_shape=jax.ShapeDtypeStruct((n, MAIN_WIDTH), F32),
        scratch_shapes=[pltpu.VMEM((d, tn), BF16)],
        compiler_params=_params("arbitrary", "arbitrary"),
    )(h2d, w_in)


def _conv_taps(x, cw_ref, t0, n_ctx, t_len, masked):
    rows = x.shape[0]
    if masked:
        t = t0 + lax.broadcasted_iota(jnp.int32, x.shape, 0)
        seg_t = jnp.where(t >= n_ctx, 1, 0)
    acc = jnp.zeros_like(x)
    for j in range(CONV_W):
        off = j - CONV_PAD_LEFT
        if off == 0:
            xs = x
        else:
            xs = pltpu.roll(x, (-off) % rows, 0)
            if masked:
                s = t + off
                seg_s = (jnp.where(s >= n_ctx, 1, 0) + jnp.where(s >= t_len, 1, 0)
                         - jnp.where(s < 0, 1, 0))
                xs = jnp.where(seg_s == seg_t, xs, 0.0)
        acc = acc + xs * cw_ref[j:j + 1, :]
    return acc


def _conv_kernel(p_ref, cw_ref, u_ref, acc_ref, *, n_ctx):
    kind = pl.program_id(1) // GDN_HEADS
    t_len = p_ref.shape[0]
    edge, win = 8, 24
    acc_ref[...] = _conv_taps(p_ref[...], cw_ref, 0, n_ctx, t_len, False)
    for r0 in (0, n_ctx - edge, n_ctx, t_len - edge):
        w0 = min(max(r0 - edge, 0), t_len - win)
        fixed = _conv_taps(p_ref[w0:w0 + win, :], cw_ref, w0, n_ctx, t_len, True)
        acc_ref[r0:r0 + edge, :] = fixed[r0 - w0:r0 - w0 + edge]
    acc = acc_ref[...]

    @pl.when(kind == 0)
    def _():
        u_ref[...] = acc

    @pl.when(kind == 3)
    def _():
        u_ref[...] = _silu(acc)

    @pl.when(jnp.logical_or(kind == 1, kind == 2))
    def _():
        y = _silu(acc)
        nrm = y * lax.rsqrt(jnp.sum(y * y, axis=-1, keepdims=True) + NORM_EPS)
        u_ref[...] = nrm * jnp.where(kind == 1, GDN_DK ** -0.5, 1.0)


def _conv(p, conv_w, l, n_ctx):
    b, t, _ = p.shape
    return pl.pallas_call(
        functools.partial(_conv_kernel, n_ctx=n_ctx),
        name="conv",
        grid=(b, CONV_CH // LANE),
        in_specs=[pl.BlockSpec((None, t, LANE), lambda bi, j: (bi, 0, j)),
                  pl.BlockSpec((None, CONV_W, LANE), lambda bi, j: (l, 0, j))],
        out_specs=pl.BlockSpec((None, t, LANE), lambda bi, j: (bi, 0, j)),
        out_shape=jax.ShapeDtypeStruct((b, t, CONV_CH), F32),
        scratch_shapes=[pltpu.VMEM((t, LANE), F32)],
        compiler_params=_params("arbitrary", "arbitrary"),
    )(p, conv_w)


LRU_LANES = 512


def _lru_kernel(u_ref, y_ref, wa_ref, ba_ref, wi_ref, bi_ref, lam_ref, o_ref,
                af_ref, bf_ref, ab_ref, bb_ref, *, n_ctx):
    t_len, width = u_ref.shape
    nblk = width // LRU_BLOCK
    tm = ROW_TILE

    def gates(i, carry):
        r0 = pl.multiple_of(i * tm, tm)
        u = u_ref[pl.ds(r0, tm), :]
        ub = u.astype(BF16)
        for d, (a_ref, b_ref) in enumerate(((af_ref, bf_ref), (ab_ref, bb_ref))):
            pre_a = jnp.concatenate(
                [jnp.dot(ub[:, n * LRU_BLOCK:(n + 1) * LRU_BLOCK], wa_ref[d, n].astype(BF16),
                         preferred_element_type=F32) for n in range(nblk)], axis=-1)
            pre_i = jnp.concatenate(
                [jnp.dot(ub[:, n * LRU_BLOCK:(n + 1) * LRU_BLOCK], wi_ref[d, n].astype(BF16),
                         preferred_element_type=F32) for n in range(nblk)], axis=-1)
            r = jax.nn.sigmoid(pre_a + ba_ref[d:d + 1, :])
            gi = jax.nn.sigmoid(pre_i + bi_ref[d:d + 1, :])
            lam = lam_ref[d:d + 1, :]
            log_sig = jnp.minimum(lam, 0.0) - jnp.log(1.0 + jnp.exp(-jnp.abs(lam)))
            log_a = LRU_C * r * log_sig
            a = jnp.exp(log_a)
            bcoef = jnp.sqrt(jnp.maximum(1.0 - jnp.exp(2.0 * log_a), 0.0)) * (gi * u)
            a_ref[pl.ds(r0, tm), :] = a
            b_ref[pl.ds(r0, tm), :] = bcoef
        return carry

    lax.fori_loop(0, t_len // tm, gates, 0)

    def scan(gi, carry):
        hf, hb = carry
        base = pl.multiple_of(gi * 8, 8)
        bbase = jnp.where(base < n_ctx, n_ctx - 1 - base, t_len + n_ctx - 1 - base)
        for j in range(8):
            tf = base + j
            tb = bbase - j
            hf = af_ref[pl.ds(tf, 1), :] * hf + bf_ref[pl.ds(tf, 1), :]
            af_ref[pl.ds(tf, 1), :] = hf
            hb = ab_ref[pl.ds(tb, 1), :] * hb + bb_ref[pl.ds(tb, 1), :]
            ab_ref[pl.ds(tb, 1), :] = hb
        return hf, hb

    zero = jnp.zeros((1, width), F32)
    lax.fori_loop(0, t_len // 8, scan, (zero, zero))
    o_ref[...] = jax.nn.gelu(y_ref[...], approximate=True) * (af_ref[...] + ab_ref[...])


def _lru(u, p, lru_w_a, lru_b_a, lru_w_i, lru_b_i, lru_lam, l, n_ctx):
    b, t, _ = u.shape
    w = LRU_LANES
    nb = w // LRU_BLOCK
    y_off = CONV_CH // w
    wspec = pl.BlockSpec((None, 2, nb, LRU_BLOCK, LRU_BLOCK), lambda bi, j: (l, 0, j, 0, 0))
    vspec = pl.BlockSpec((None, 2, w), lambda bi, j: (l, 0, j))
    return pl.pallas_call(
        functools.partial(_lru_kernel, n_ctx=n_ctx),
        name="lru",
        grid=(b, LRU_WIDTH // w),
        in_specs=[pl.BlockSpec((None, t, w), lambda bi, j: (bi, 0, j)),
                  pl.BlockSpec((None, t, w), lambda bi, j: (bi, 0, y_off + j)),
                  wspec, vspec, wspec, vspec, vspec],
        out_specs=pl.BlockSpec((None, t, w), lambda bi, j: (bi, 0, j)),
        out_shape=jax.ShapeDtypeStruct((b, t, LRU_WIDTH), F32),
        scratch_shapes=[pltpu.VMEM((t, w), F32)] * 4,
        compiler_params=_params("arbitrary", "arbitrary"),
    )(u, p, lru_w_a, lru_b_a, lru_w_i, lru_b_i, lru_lam)


def _split(a):
    hi = a.astype(BF16).astype(F32)
    return hi, a - hi


def _lhs3(a):
    hi, lo = _split(a)
    return jnp.concatenate([hi, lo, hi], axis=1).astype(BF16)


def _rhs3(b):
    hi, lo = _split(b)
    return jnp.concatenate([hi, hi, lo], axis=0).astype(BF16)


def _gdn_kernel(qf_ref, kf_ref, vf_ref, gf_ref, qb_ref, kb_ref, vb_ref, gb_ref, alog_ref, dtb_ref,
                of_ref, ob_ref, s_ref):
    c, h_n, dk, dv = CHUNK, GDN_HEADS, GDN_DK, GDN_DV

    @pl.when(pl.program_id(1) == 0)
    def _():
        s_ref[...] = jnp.zeros_like(s_ref)

    ri = lax.broadcasted_iota(jnp.int32, (c, c), 0)
    ci = lax.broadcasted_iota(jnp.int32, (c, c), 1)
    dirs = ((qf_ref, kf_ref, vf_ref, gf_ref, of_ref), (qb_ref, kb_ref, vb_ref, gb_ref, ob_ref))

    chains = []
    for d, (q_ref, k_ref, v_ref, gt_ref, o_ref) in enumerate(dirs):
        gt = gt_ref[...]
        beta = jax.nn.sigmoid(gt[:, d * h_n:(d + 1) * h_n])
        xa = gt[:, (2 + d) * h_n:(3 + d) * h_n] + dtb_ref[d:d + 1, :]
        softplus = jnp.maximum(xa, 0.0) + jnp.log(1.0 + jnp.exp(-jnp.abs(xa)))
        g = -jnp.exp(alog_ref[d:d + 1, :]) * softplus
        incl = (ri <= ci) if d else (ri >= ci)
        strict = (ri < ci) if d else (ri > ci)
        ones_incl = jnp.where(incl, 1.0, 0.0)
        cum = jnp.dot(ones_incl, g, precision=HI, preferred_element_type=F32)
        cum_t = lax.dot_general(g, ones_incl, (((0,), (1,)), ((), ())), precision=HI,
                                preferred_element_type=F32)
        total = cum[0:1, :] if d else cum[c - 1:c, :]
        e_cum = jnp.exp(cum)
        e_rest = jnp.exp(total - cum)
        e_total = jnp.exp(total)
        for h in range(h_n):
            sl = slice(h * dk, (h + 1) * dk)
            k = k_ref[:, sl]
            kb = k.astype(BF16)
            diff = jnp.where(incl, cum[:, h:h + 1] - cum_t[h:h + 1, :], 0.0)
            gamma = jnp.where(incl, jnp.exp(diff), 0.0)
            bcol = beta[:, h:h + 1]
            kk = lax.dot_general(kb, kb, NT_DIMS, preferred_element_type=F32)
            chains.append(dict(
                d=d, h=h, q_ref=q_ref, k_ref=k_ref, o_ref=o_ref, gamma=gamma,
                neg=jnp.where(strict, -(bcol * kk * gamma), 0.0),
                x=jnp.concatenate([bcol * v_ref[:, sl], bcol * k * e_cum[:, h:h + 1]], axis=-1),
                e_cum=e_cum[:, h:h + 1], e_rest=e_rest[:, h:h + 1], e_total=e_total[:, h:h + 1]))

    for level in range(6):
        for ch in chains:
            a3 = _lhs3(ch["neg"])
            ch["x"] = ch["x"] + jnp.dot(a3, _rhs3(ch["x"]), preferred_element_type=F32)
            if level < 5:
                ch["neg"] = jnp.dot(a3, _rhs3(ch["neg"]), preferred_element_type=F32)

    for ch in chains:
        d, h = ch["d"], ch["h"]
        sl = slice(h * dk, (h + 1) * dk)
        q = ch["q_ref"][:, sl]
        k = ch["k_ref"][:, sl]
        u = ch["x"][:, :dv]
        w = ch["x"][:, dv:]
        qk = lax.dot_general(q.astype(BF16), k.astype(BF16), NT_DIMS,
                             preferred_element_type=F32) * ch["gamma"]
        q_dec = q * ch["e_cum"]
        k_dec = k * ch["e_rest"]
        s = s_ref[d, h]
        ws = jnp.dot(jnp.concatenate([w, q_dec], axis=0).astype(BF16), s.astype(BF16),
                     preferred_element_type=F32)
        v_new = u - ws[:c]
        vb = v_new.astype(BF16)
        ch["o_ref"][:, h * dv:(h + 1) * dv] = ws[c:] + jnp.dot(qk.astype(BF16), vb,
                                                               preferred_element_type=F32)
        s_ref[d, h] = ch["e_total"] * s + lax.dot_general(
            k_dec.astype(BF16), vb, TN_DIMS, preferred_element_type=F32)


def _gdn(u, pg, gdn_a_log, gdn_dt_bias, l, n_ctx):
    b, t, _ = u.shape
    nc = t // CHUNK
    ncc = n_ctx // CHUNK

    def rev(s):
        return jnp.where(s < ncc, ncc - 1 - s, nc + ncc - 1 - s)

    def specs(chunk):
        qkv = [pl.BlockSpec((None, CHUNK, QK_WIDTH), lambda bi, s, col=col: (bi, chunk(s), col))
               for col in (1, 2, 3)]
        return qkv + [pl.BlockSpec((None, CHUNK, N_GATE), lambda bi, s: (bi, chunk(s), 0))]

    fwd = lambda s: s
    small = pl.BlockSpec((None, 2, GDN_HEADS), lambda bi, s: (l, 0, 0))
    out = jax.ShapeDtypeStruct((b, t, V_WIDTH), F32)
    return pl.pallas_call(
        _gdn_kernel,
        name="gdn",
        grid=(b, nc),
        in_specs=specs(fwd) + specs(rev) + [small, small],
        out_specs=[pl.BlockSpec((None, CHUNK, V_WIDTH), lambda bi, s: (bi, s, 0)),
                   pl.BlockSpec((None, CHUNK, V_WIDTH), lambda bi, s: (bi, rev(s), 0))],
        out_shape=[out, out],
        scratch_shapes=[pltpu.VMEM((2, GDN_HEADS, GDN_DK, GDN_DV), F32)],
        compiler_params=_params("arbitrary", "arbitrary"),
    )(u, u, u, pg, u, u, u, pg, gdn_a_log, gdn_dt_bias)


def _outproj_kernel(ylru_ref, of_ref, ob_ref, z_ref, x_ref, mod_ref, gn_ref, w_ref, o_ref):
    o = of_ref[...] + ob_ref[...]
    z = z_ref[...]
    parts = [ylru_ref[...]]
    for h in range(GDN_HEADS):
        sl = slice(h * GDN_DV, (h + 1) * GDN_DV)
        oh = o[:, sl]
        ms = jnp.mean(oh * oh, axis=-1, keepdims=True)
        parts.append(oh * lax.rsqrt(ms + NORM_EPS) * gn_ref[...] * _silu(z[:, sl]))
    feats = jnp.concatenate(parts, axis=-1).astype(BF16)
    y = jnp.dot(feats, w_ref[...], preferred_element_type=F32)
    o_ref[...] = x_ref[...] + mod_ref[2:3, :] * y


def _outproj(y_lru, o_f, o_b, p, x, mods, gdn_norm_g, w_out_bf, l, n_ctx):
    b, t, d = x.shape
    tm = ROW_TILE
    nct = n_ctx // tm
    z_off = (CONV_CH + LRU_WIDTH) // V_WIDTH
    row = lambda bi, i: (bi, i, 0)
    return pl.pallas_call(
        _outproj_kernel,
        name="outproj",
        grid=(b, t // tm),
        in_specs=[pl.BlockSpec((None, tm, LRU_WIDTH), row),
                  pl.BlockSpec((None, tm, V_WIDTH), row),
                  pl.BlockSpec((None, tm, V_WIDTH), row),
                  pl.BlockSpec((None, tm, V_WIDTH), lambda bi, i: (bi, i, z_off)),
                  pl.BlockSpec((None, tm, d), row),
                  pl.BlockSpec((None, None, N_MOD, d),
                               lambda bi, i: (l, jnp.where(i < nct, b, bi), 0, 0)),
                  pl.BlockSpec((None, 1, GDN_DV), lambda bi, i: (l, 0, 0)),
                  pl.BlockSpec((None, LRU_WIDTH + V_WIDTH, d), lambda bi, i: (l, 0, 0))],
        out_specs=pl.BlockSpec((None, tm, d), row),
        out_shape=jax.ShapeDtypeStruct((b, t, d), F32),
        compiler_params=_params("arbitrary", "arbitrary"),
    )(y_lru, o_f, o_b, p, x, mods, gdn_norm_g, w_out_bf)


def _router_kernel(x_ref, mod_ref, g_ref, wr_ref, rb_ref, h_ref, idx_ref, wt_ref, rank_ref,
                   cnt_ref, run_ref, blk_ref):
    first = jnp.logical_and(pl.program_id(0) == 0, pl.program_id(1) == 0)

    @pl.when(first)
    def _():
        run_ref[...] = jnp.zeros_like(run_ref)

    h = _norm_mod(x_ref[...], g_ref[...], mod_ref[3:4, :], mod_ref[4:5, :])
    scores = jax.nn.sigmoid(jnp.dot(h, wr_ref[...], precision=HI, preferred_element_type=F32))
    tm, n_e = scores.shape
    _to_blocks(h, blk_ref)
    h_ref[...] = blk_ref[...].astype(BF16)
    lane = lax.broadcasted_iota(jnp.int32, (tm, n_e), 1)
    kl = lax.broadcasted_iota(jnp.int32, (tm, TOP_K), 1)
    biased = scores + rb_ref[...]
    idx = jnp.zeros((tm, TOP_K), jnp.int32)
    sel = jnp.zeros((tm, TOP_K), F32)
    maskf = jnp.zeros((tm, n_e), F32)
    onehots = []
    for k in range(TOP_K):
        m = jnp.max(biased, axis=-1, keepdims=True)
        ik = jnp.min(jnp.where(biased == m, lane, n_e), axis=-1, keepdims=True)
        oh = lane == ik
        onehots.append(oh)
        sk = jnp.sum(jnp.where(oh, scores, 0.0), axis=-1, keepdims=True)
        idx = jnp.where(kl == k, ik, idx)
        sel = jnp.where(kl == k, sk, sel)
        maskf = jnp.where(oh, 1.0, maskf)
        biased = jnp.where(oh, -jnp.inf, biased)
    wt_ref[...] = sel / jnp.sum(sel, axis=-1, keepdims=True) * ROUTED_SCALE
    idx_ref[...] = idx

    ri = lax.broadcasted_iota(jnp.int32, (tm, tm), 0)
    ci = lax.broadcasted_iota(jnp.int32, (tm, tm), 1)
    before = jnp.where(ri > ci, 1.0, 0.0).astype(BF16)
    prior = jnp.dot(before, maskf.astype(BF16), preferred_element_type=F32) + run_ref[...]
    rank = jnp.zeros((tm, TOP_K), F32)
    for k in range(TOP_K):
        rk = jnp.sum(jnp.where(onehots[k], prior, 0.0), axis=-1, keepdims=True)
        rank = jnp.where(kl == k, rk, rank)
    rank_ref[...] = rank.astype(jnp.int32)
    run_ref[...] = run_ref[...] + jnp.sum(maskf, axis=0, keepdims=True)
    cnt_ref[...] = run_ref[...]


def _router(x, mods, g, w_router, router_bias, l, n_ctx):
    b, t, d = x.shape
    n_e = w_router.shape[-1]
    tm = ROW_TILE
    nt = t // tm
    nct = n_ctx // tm
    nb = d // LANE
    row = lambda bi, i: (bi, i, 0)
    return pl.pallas_call(
        _router_kernel,
        name="router",
        grid=(b, nt),
        in_specs=[pl.BlockSpec((None, tm, d), row),
                  pl.BlockSpec((None, None, N_MOD, d),
                               lambda bi, i: (l, jnp.where(i < nct, b, bi), 0, 0)),
                  pl.BlockSpec((None, 1, d), lambda bi, i: (l, 0, 0)),
                  pl.BlockSpec((None, d, n_e), lambda bi, i: (l, 0, 0)),
                  pl.BlockSpec((None, 1, n_e), lambda bi, i: (l, 0, 0))],
        out_specs=[pl.BlockSpec((tm * nb, LANE), lambda bi, i: (bi * nt + i, 0)),
                   pl.BlockSpec((None, tm, TOP_K), row),
                   pl.BlockSpec((None, tm, TOP_K), row),
                   pl.BlockSpec((None, tm, TOP_K), row),
                   pl.BlockSpec((1, n_e), lambda bi, i: (0, 0))],
        out_shape=[jax.ShapeDtypeStruct((b * t * nb, LANE), BF16),
                   jax.ShapeDtypeStruct((b, t, TOP_K), jnp.int32),
                   jax.ShapeDtypeStruct((b, t, TOP_K), F32),
                   jax.ShapeDtypeStruct((b, t, TOP_K), jnp.int32),
                   jax.ShapeDtypeStruct((1, n_e), F32)],
        scratch_shapes=[pltpu.VMEM((1, n_e), F32), pltpu.VMEM((tm * nb, LANE), F32)],
        compiler_params=_params("arbitrary", "arbitrary"),
    )(x, mods, g, w_router, router_bias)


DISPATCH_TOKENS = 256


def _dispatch_kernel(dest_ref, h_ref, xs_in_ref, xs_ref, sem, *, nb):
    del xs_in_ref

    def issue(r, carry):
        src = h_ref.at[pl.ds(pl.multiple_of(r * nb, nb), nb)]
        for k in range(TOP_K):
            slot = dest_ref[r * TOP_K + k]
            pltpu.make_async_copy(src, xs_ref.at[pl.ds(pl.multiple_of(slot * nb, nb), nb)],
                                  sem).start()
        return carry

    lax.fori_loop(0, DISPATCH_TOKENS, issue, 0)
    for k in range(TOP_K):
        pltpu.make_async_copy(h_ref, xs_ref.at[pl.ds(0, DISPATCH_TOKENS * nb)], sem).wait()


def _dispatch(h_blocks, dest_flat, n_slots, nb):
    n = h_blocks.shape[0] // nb
    xs0 = jnp.zeros((n_slots * nb, LANE), h_blocks.dtype)
    return pl.pallas_call(
        functools.partial(_dispatch_kernel, nb=nb),
        name="dispatch",
        grid=(n // DISPATCH_TOKENS,),
        in_specs=[pl.BlockSpec((DISPATCH_TOKENS * TOP_K,), lambda i: (i,),
                               memory_space=pltpu.SMEM),
                  pl.BlockSpec((DISPATCH_TOKENS * nb, LANE), lambda i: (i, 0)),
                  pl.BlockSpec(memory_space=pl.ANY)],
        out_specs=pl.BlockSpec(memory_space=pl.ANY),
        out_shape=jax.ShapeDtypeStruct((n_slots * nb, LANE), h_blocks.dtype),
        scratch_shapes=[pltpu.SemaphoreType.DMA(())],
        input_output_aliases={2: 0},
        compiler_params=_params("arbitrary"),
    )(dest_flat, h_blocks, xs0)


EXPERT_TILE = 256


def _expert_kernel(te_ref, na_ref, slot_ref, next_ref, x_ref, wg_hbm, wu_hbm, wd_hbm, y_ref,
                   wgf_ref, wuf_ref, wdf_ref, wgb_ref, wub_ref, wdb_ref, blk_ref, wsem, *, layer):
    i = pl.program_id(0)
    active = i < na_ref[0]
    changed = jnp.logical_or(i == 0, te_ref[i] != te_ref[jnp.maximum(i - 1, 0)])
    slot = slot_ref[i]

    def weight_copies(e, s):
        return [pltpu.make_async_copy(hbm.at[layer, e], buf.at[s], wsem.at[s])
                for hbm, buf in ((wg_hbm, wgf_ref), (wu_hbm, wuf_ref), (wd_hbm, wdf_ref))]

    @pl.when(jnp.logical_and(active, i == 0))
    def _():
        for cp in weight_copies(te_ref[0], 0):
            cp.start()

    @pl.when(jnp.logical_and(active, changed))
    def _():
        for cp in weight_copies(te_ref[i], slot):
            cp.wait()

        @pl.when(next_ref[i] >= 0)
        def _():
            for cp in weight_copies(next_ref[i], 1 - slot):
                cp.start()

        wgb_ref[...] = wgf_ref[slot].astype(BF16)
        wub_ref[...] = wuf_ref[slot].astype(BF16)
        wdb_ref[...] = wdf_ref[slot].astype(BF16)

    @pl.when(active)
    def _():
        blk_ref[...] = x_ref[...].astype(F32)
        xb = _from_blocks(blk_ref, EXPERT_TILE).astype(BF16)
        hid = _silu(jnp.dot(xb, wgb_ref[...], preferred_element_type=F32)) * jnp.dot(
            xb, wub_ref[...], preferred_element_type=F32)
        y_ref[...] = jnp.dot(hid.astype(BF16), wdb_ref[...], preferred_element_type=F32)

    @pl.when(jnp.logical_not(active))
    def _():
        y_ref[...] = jnp.zeros_like(y_ref)


def _experts(xs, tile_expert, n_active, weight_slot, next_expert, w_gate, w_up, w_down, l):
    d, f = w_gate.shape[-2:]
    nb = d // LANE
    n_slots = xs.shape[0] // nb
    tm = EXPERT_TILE
    hbm = pl.BlockSpec(memory_space=pl.ANY)
    grid_spec = pltpu.PrefetchScalarGridSpec(
        num_scalar_prefetch=4,
        grid=(n_slots // tm,),
        in_specs=[pl.BlockSpec((tm * nb, LANE), lambda i, *_: (i, 0)), hbm, hbm, hbm],
        out_specs=pl.BlockSpec((tm, d), lambda i, *_: (i, 0)),
        scratch_shapes=[pltpu.VMEM((2, d, f), F32), pltpu.VMEM((2, d, f), F32),
                        pltpu.VMEM((2, f, d), F32),
                        pltpu.VMEM((d, f), BF16), pltpu.VMEM((d, f), BF16),
                        pltpu.VMEM((f, d), BF16), pltpu.VMEM((tm * nb, LANE), F32),
                        pltpu.SemaphoreType.DMA((2,))],
    )
    return pl.pallas_call(
        functools.partial(_expert_kernel, layer=l),
        name="experts",
        grid_spec=grid_spec,
        out_shape=jax.ShapeDtypeStruct((n_slots, d), F32),
        compiler_params=_params("arbitrary"),
    )(tile_expert, n_active, weight_slot, next_expert, xs, w_gate, w_up, w_down)


COMBINE_TOKENS = 128


def _combine_kernel(dest_ref, ys_ref, x_ref, g_ref, wt_ref, mod_ref, sg_ref, su_ref, sd_ref,
                    o_ref, buf_ref, sem):
    tm = x_ref.shape[0]

    def issue(r, carry):
        for k in range(TOP_K):
            slot = dest_ref[r * TOP_K + k]
            pltpu.make_async_copy(ys_ref.at[pl.ds(slot, 1)], buf_ref.at[pl.ds(k * tm + r, 1)],
                                  sem).start()
        return carry

    lax.fori_loop(0, tm, issue, 0)
    x = x_ref[...]
    hb = _norm_mod(x, g_ref[...], mod_ref[3:4, :], mod_ref[4:5, :]).astype(BF16)
    hid = _silu(jnp.dot(hb, sg_ref[...], preferred_element_type=F32)) * jnp.dot(
        hb, su_ref[...], preferred_element_type=F32)
    y = jnp.dot(hid.astype(BF16), sd_ref[...], preferred_element_type=F32)
    pltpu.make_async_copy(ys_ref.at[pl.ds(0, TOP_K * tm)], buf_ref, sem).wait()
    wt = wt_ref[...]
    for k in range(TOP_K):
        y = y + wt[:, k:k + 1] * buf_ref[pl.ds(k * tm, tm), :]
    o_ref[...] = x + mod_ref[5:6, :] * y


def _combine(ys, dest_flat, wts, x, g, mods, ws_gate_bf, ws_up_bf, ws_down_bf, l, n_ctx):
    b, t, d = x.shape
    tm = COMBINE_TOKENS
    nt = t // tm
    nct = n_ctx // tm
    fs = ws_gate_bf.shape[-1]
    row = lambda bi, i: (bi, i, 0)
    return pl.pallas_call(
        _combine_kernel,
        name="combine",
        grid=(b, nt),
        in_specs=[pl.BlockSpec((tm * TOP_K,), lambda bi, i: (bi * nt + i,),
                               memory_space=pltpu.SMEM),
                  pl.BlockSpec(memory_space=pl.ANY),
                  pl.BlockSpec((None, tm, d), row),
                  pl.BlockSpec((None, 1, d), lambda bi, i: (l, 0, 0)),
                  pl.BlockSpec((None, tm, TOP_K), row),
                  pl.BlockSpec((None, None, N_MOD, d),
                               lambda bi, i: (l, jnp.where(i < nct, b, bi), 0, 0)),
                  pl.BlockSpec((None, d, fs), lambda bi, i: (l, 0, 0)),
                  pl.BlockSpec((None, d, fs), lambda bi, i: (l, 0, 0)),
                  pl.BlockSpec((None, fs, d), lambda bi, i: (l, 0, 0))],
        out_specs=pl.BlockSpec((None, tm, d), row),
        out_shape=jax.ShapeDtypeStruct((b, t, d), F32),
        scratch_shapes=[pltpu.VMEM((TOP_K * tm, d), F32), pltpu.SemaphoreType.DMA(())],
        compiler_params=_params("arbitrary", "arbitrary"),
    )(dest_flat, ys, x, g, wts, mods, ws_gate_bf, ws_up_bf, ws_down_bf)


def _final_kernel(x_ref, g_ref, o_ref):
    x = x_ref[...]
    ms = jnp.mean(x * x, axis=-1, keepdims=True)
    o_ref[...] = x * lax.rsqrt(ms + NORM_EPS) * g_ref[...]


def _final_norm(x, g_final, n_ctx):
    b, t, d = x.shape
    tm = ROW_TILE
    nct = n_ctx // tm
    return pl.pallas_call(
        _final_kernel,
        name="final_norm",
        grid=(b, (t - n_ctx) // tm),
        in_specs=[pl.BlockSpec((None, tm, d), lambda bi, i: (bi, i + nct, 0)),
                  pl.BlockSpec((1, d), lambda bi, i: (0, 0))],
        out_specs=pl.BlockSpec((None, tm, d), lambda bi, i: (bi, i, 0)),
        out_shape=jax.ShapeDtypeStruct((b, t - n_ctx, d), F32),
        compiler_params=_params("arbitrary", "arbitrary"),
    )(x, g_final.reshape(1, d))


def _latent_permute(xs, n_ctx, to_col_major):
    b, t, d = xs.shape
    rows = (t - n_ctx) // GRID_W
    lat = xs[:, n_ctx:]
    if to_col_major:
        lat = lat.reshape(b, rows, GRID_W, d).transpose(0, 2, 1, 3)
    else:
        lat = lat.reshape(b, GRID_W, rows, d).transpose(0, 2, 1, 3)
    return jnp.concatenate([xs[:, :n_ctx], lat.reshape(b, t - n_ctx, d)], axis=1)


def _routing_plan(idx, rank, counts, n_tiles):
    n_e = counts.shape[-1]
    cnt = counts.reshape(n_e).astype(jnp.int32)
    tiles = (cnt + EXPERT_TILE - 1) // EXPERT_TILE
    tile_end = jnp.cumsum(tiles)
    start = (tile_end - tiles) * EXPERT_TILE
    onehot = idx[..., None] == jnp.arange(n_e, dtype=jnp.int32)
    dest = rank + jnp.sum(jnp.where(onehot, start, 0), axis=-1)
    tile_ids = jnp.arange(n_tiles, dtype=jnp.int32)
    tile_expert = jnp.sum(tile_ids[:, None] >= tile_end[None, :], axis=-1).astype(jnp.int32)
    n_active = tile_end[-1]
    tile_expert = jnp.where(tile_ids < n_active, jnp.minimum(tile_expert, n_e - 1),
                            jnp.sum(jnp.where(tile_ids == n_active - 1, tile_expert, 0)))
    changed = jnp.concatenate([jnp.ones((1,), bool), tile_expert[1:] != tile_expert[:-1]])
    weight_slot = ((jnp.cumsum(changed) - 1) % 2).astype(jnp.int32)
    next_change = lax.cummin(jnp.where(changed, tile_ids, n_tiles), axis=0, reverse=True)
    next_change = jnp.concatenate([next_change[1:], jnp.full((1,), n_tiles, jnp.int32)])
    next_expert = jnp.where(next_change < n_tiles,
                            tile_expert[jnp.minimum(next_change, n_tiles - 1)], -1)
    return (dest.reshape(-1).astype(jnp.int32), tile_expert.astype(jnp.int32),
            n_active.reshape(1).astype(jnp.int32), weight_slot, next_expert.astype(jnp.int32))


def kernel(x, c, ctx, c_ctx, w_mod, b_mod, g_mix, g_ffn, w_in, conv_w, lru_w_a, lru_b_a, lru_w_i,
           lru_b_i, lru_lam, gdn_a_log, gdn_dt_bias, gdn_norm_g, w_out, w_router, router_bias,
           w_gate, w_up, w_down, ws_gate, ws_up, ws_down, g_final):
    b, seq, d = x.shape
    n_ctx = ctx.shape[1]
    t = n_ctx + seq
    depth = w_mod.shape[0]
    n_e = w_router.shape[-1]
    assert n_ctx % ROW_TILE == 0 and seq % ROW_TILE == 0 and b < 8
    n_tiles = (b * t * TOP_K) // EXPERT_TILE + n_e

    mods = _modulation(c, c_ctx, w_mod, b_mod)
    g_mix3 = g_mix.reshape(depth, 1, d)
    g_ffn3 = g_ffn.reshape(depth, 1, d)
    gn3 = gdn_norm_g.reshape(depth, 1, GDN_DV)
    rb3 = router_bias.reshape(depth, 1, n_e)
    w_out_bf = w_out.astype(BF16)
    ws_gate_bf, ws_up_bf, ws_down_bf = (w.astype(BF16) for w in (ws_gate, ws_up, ws_down))

    xs = jnp.concatenate([ctx, x], axis=1)
    for l in range(depth):
        col_major = l % 2 == 1
        if col_major:
            xs = _latent_permute(xs, n_ctx, True)
        h, pg = _inprep(xs, mods, g_mix3, w_in[l, :, MAIN_WIDTH:].astype(BF16), l, n_ctx)
        p = _inproj(h.reshape(b * t, d), w_in, l).reshape(b, t, MAIN_WIDTH)
        u = _conv(p, conv_w, l, n_ctx)
        y_lru = _lru(u, p, lru_w_a, lru_b_a, lru_w_i, lru_b_i, lru_lam, l, n_ctx)
        o_f, o_b = _gdn(u, pg, gdn_a_log, gdn_dt_bias, l, n_ctx)
        xs = _outproj(y_lru, o_f, o_b, p, xs, mods, gn3, w_out_bf, l, n_ctx)
        if col_major:
            xs = _latent_permute(xs, n_ctx, False)

        h2, idx, wts, rank, counts = _router(xs, mods, g_ffn3, w_router, rb3, l, n_ctx)
        dest, tile_expert, n_active, weight_slot, next_expert = _routing_plan(idx, rank, counts,
                                                                              n_tiles)
        xsorted = _dispatch(h2, dest, n_tiles * EXPERT_TILE, d // LANE)
        ysorted = _experts(xsorted, tile_expert, n_active, weight_slot, next_expert, w_gate, w_up,
                           w_down, l)
        xs = _combine(ysorted, dest, wts, xs, g_ffn3, mods, ws_gate_bf, ws_up_bf, ws_down_bf, l,
                      n_ctx)
    return _final_norm(xs, g_final, n_ctx)
```

```python
import functools

import jax
import jax.numpy as jnp
from jax import lax
from jax.experimental import pallas as pl
from jax.experimental.pallas import tpu as pltpu

F32 = jnp.float32
BF16 = jnp.bfloat16
HI = lax.Precision.HIGHEST

GRID_W = 64
N_MOD = 6
NORM_EPS = 1e-6
LRU_WIDTH = 1024
LRU_BLOCK = 128
LRU_C = 8.0
GDN_HEADS = 8
GDN_DK = 128
GDN_DV = 128
QK_WIDTH = GDN_HEADS * GDN_DK
V_WIDTH = GDN_HEADS * GDN_DV
CHUNK = 64
CONV_W = 4
CONV_PAD_LEFT = 2
CONV_CH = LRU_WIDTH + 2 * QK_WIDTH + V_WIDTH
MAIN_WIDTH = CONV_CH + LRU_WIDTH + V_WIDTH
N_GATE = 4 * GDN_HEADS
TOP_K = 8
ROUTED_SCALE = 2.5

LANE = 128
ROW_TILE = 256
VMEM_LIMIT = 56 * 1024 * 1024

NT_DIMS = (((1,), (1,)), ((), ()))
TN_DIMS = (((0,), (0,)), ((), ()))


def _params(*sem):
    return pltpu.CompilerParams(dimension_semantics=sem, vmem_limit_bytes=VMEM_LIMIT)


def _silu(x):
    return x * jax.nn.sigmoid(x)


def _norm_mod(x, g, shift, scale):
    ms = jnp.mean(x * x, axis=-1, keepdims=True)
    return (x * lax.rsqrt(ms + NORM_EPS) * g) * (1.0 + scale) + shift


def _to_blocks(y, blk_ref):
    tm, d = y.shape
    nb = d // LANE
    for s in range(nb):
        blk_ref[pl.ds(s, tm, stride=nb), :] = y[:, s * LANE:(s + 1) * LANE]


def _from_blocks(blk_ref, tm):
    nb = blk_ref.shape[0] // tm
    return jnp.concatenate([blk_ref[pl.ds(s, tm, stride=nb), :] for s in range(nb)], axis=-1)


def _mod_kernel(s_ref, w_ref, b_ref, o_ref):
    s = _silu(s_ref[...])
    o_ref[...] = jnp.dot(s.astype(BF16), w_ref[...].astype(BF16),
                         preferred_element_type=F32) + b_ref[...]


def _modulation(c, c_ctx, w_mod, b_mod):
    depth, d, nd = w_mod.shape
    b = c.shape[0]
    rows = jnp.concatenate([c, c_ctx[None], jnp.zeros((8 - b - 1, d), F32)], axis=0)
    tn = 1024
    out = pl.pallas_call(
        _mod_kernel,
        name="modulation",
        grid=(depth, nd // tn),
        in_specs=[pl.BlockSpec((8, d), lambda l, j: (0, 0)),
                  pl.BlockSpec((None, d, tn), lambda l, j: (l, 0, j)),
                  pl.BlockSpec((None, 1, tn), lambda l, j: (l, 0, j))],
        out_specs=pl.BlockSpec((None, 8, tn), lambda l, j: (l, 0, j)),
        out_shape=jax.ShapeDtypeStruct((depth, 8, nd), F32),
        compiler_params=_params("arbitrary", "arbitrary"),
    )(rows, w_mod, b_mod.reshape(depth, 1, nd))
    return out.reshape(depth, 8, N_MOD, d)


def _inprep_kernel(x_ref, mod_ref, g_ref, wg_ref, h_ref, pg_ref):
    h = _norm_mod(x_ref[...], g_ref[...], mod_ref[0:1, :], mod_ref[1:2, :])
    hb = h.astype(BF16)
    h_ref[...] = hb
    pg_ref[...] = jnp.dot(hb, wg_ref[...], preferred_element_type=F32)


def _inprep(x, mods, g, w_gatecols, l, n_ctx):
    b, t, d = x.shape
    tm = ROW_TILE
    nct = n_ctx // tm
    return pl.pallas_call(
        _inprep_kernel,
        name="inprep",
        grid=(b, t // tm),
        in_specs=[pl.BlockSpec((None, tm, d), lambda bi, i: (bi, i, 0)),
                  pl.BlockSpec((None, None, N_MOD, d),
                               lambda bi, i: (l, jnp.where(i < nct, b, bi), 0, 0)),
                  pl.BlockSpec((None, 1, d), lambda bi, i: (l, 0, 0)),
                  pl.BlockSpec((d, N_GATE), lambda bi, i: (0, 0))],
        out_specs=[pl.BlockSpec((None, tm, d), lambda bi, i: (bi, i, 0)),
                   pl.BlockSpec((None, tm, N_GATE), lambda bi, i: (bi, i, 0))],
        out_shape=[jax.ShapeDtypeStruct((b, t, d), BF16),
                   jax.ShapeDtypeStruct((b, t, N_GATE), F32)],
        compiler_params=_params("arbitrary", "arbitrary"),
    )(x, mods, g, w_gatecols)


def _matmul_kernel(h_ref, w_ref, o_ref, wb_ref):
    @pl.when(pl.program_id(1) == 0)
    def _():
        wb_ref[...] = w_ref[...].astype(BF16)

    o_ref[...] = jnp.dot(h_ref[...], wb_ref[...], preferred_element_type=F32)


def _inproj(h2d, w_in, l):
    n, d = h2d.shape
    tn = 1024
    tm = 1024 if n % 1024 == 0 else 512
    return pl.pallas_call(
        _matmul_kernel,
        name="inproj",
        grid=(MAIN_WIDTH // tn, n // tm),
        in_specs=[pl.BlockSpec((tm, d), lambda j, i: (i, 0)),
                  pl.BlockSpec((None, d, tn), lambda j, i: (l, 0, j))],
        out_specs=pl.BlockSpec((tm, tn), lambda j, i: (i, j)),
        out_shape=jax.ShapeDtypeStruct((n, MAIN_WIDTH), F32),
        scratch_shapes=[pltpu.VMEM((d, tn), BF16)],
        compiler_params=_params("arbitrary", "arbitrary"),
    )(h2d, w_in)


def _conv_taps(x, cw_ref, t0, n_ctx, t_len, masked):
    rows = x.shape[0]
    if masked:
        t = t0 + lax.broadcasted_iota(jnp.int32, x.shape, 0)
        seg_t = jnp.where(t >= n_ctx, 1, 0)
    acc = jnp.zeros_like(x)
    for j in range(CONV_W):
        off = j - CONV_PAD_LEFT
        if off == 0:
            xs = x
        else:
            xs = pltpu.roll(x, (-off) % rows, 0)
            if masked:
                s = t + off
                seg_s = (jnp.where(s >= n_ctx, 1, 0) + jnp.where(s >= t_len, 1, 0)
                         - jnp.where(s < 0, 1, 0))
                xs = jnp.where(seg_s == seg_t, xs, 0.0)
        acc = acc + xs * cw_ref[j:j + 1, :]
    return acc


def _conv_kernel(p_ref, cw_ref, u_ref, acc_ref, *, n_ctx):
    kind = pl.program_id(1) // GDN_HEADS
    t_len = p_ref.shape[0]
    edge, win = 8, 24
    acc_ref[...] = _conv_taps(p_ref[...], cw_ref, 0, n_ctx, t_len, False)
    for r0 in (0, n_ctx - edge, n_ctx, t_len - edge):
        w0 = min(max(r0 - edge, 0), t_len - win)
        fixed = _conv_taps(p_ref[w0:w0 + win, :], cw_ref, w0, n_ctx, t_len, True)
        acc_ref[r0:r0 + edge, :] = fixed[r0 - w0:r0 - w0 + edge]
    acc = acc_ref[...]

    @pl.when(kind == 0)
    def _():
        u_ref[...] = acc

    @pl.when(kind == 3)
    def _():
        u_ref[...] = _silu(acc)

    @pl.when(jnp.logical_or(kind == 1, kind == 2))
    def _():
        y = _silu(acc)
        nrm = y * lax.rsqrt(jnp.sum(y * y, axis=-1, keepdims=True) + NORM_EPS)
        u_ref[...] = nrm * jnp.where(kind == 1, GDN_DK ** -0.5, 1.0)


def _conv(p, conv_w, l, n_ctx):
    b, t, _ = p.shape
    return pl.pallas_call(
        functools.partial(_conv_kernel, n_ctx=n_ctx),
        name="conv",
        grid=(b, CONV_CH // LANE),
        in_specs=[pl.BlockSpec((None, t, LANE), lambda bi, j: (bi, 0, j)),
                  pl.BlockSpec((None, CONV_W, LANE), lambda bi, j: (l, 0, j))],
        out_specs=pl.BlockSpec((None, t, LANE), lambda bi, j: (bi, 0, j)),
        out_shape=jax.ShapeDtypeStruct((b, t, CONV_CH), F32),
        scratch_shapes=[pltpu.VMEM((t, LANE), F32)],
        compiler_params=_params("arbitrary", "arbitrary"),
    )(p, conv_w)


LRU_LANES = 512


def _lru_kernel(u_ref, y_ref, wa_ref, ba_ref, wi_ref, bi_ref, lam_ref, o_ref,
                af_ref, bf_ref, ab_ref, bb_ref, *, n_ctx):
    t_len, width = u_ref.shape
    nblk = width // LRU_BLOCK
    tm = ROW_TILE

    def gates(i, carry):
        r0 = pl.multiple_of(i * tm, tm)
        u = u_ref[pl.ds(r0, tm), :]
        ub = u.astype(BF16)
        for d, (a_ref, b_ref) in enumerate(((af_ref, bf_ref), (ab_ref, bb_ref))):
            pre_a = jnp.concatenate(
                [jnp.dot(ub[:, n * LRU_BLOCK:(n + 1) * LRU_BLOCK], wa_ref[d, n].astype(BF16),
                         preferred_element_type=F32) for n in range(nblk)], axis=-1)
            pre_i = jnp.concatenate(
                [jnp.dot(ub[:, n * LRU_BLOCK:(n + 1) * LRU_BLOCK], wi_ref[d, n].astype(BF16),
                         preferred_element_type=F32) for n in range(nblk)], axis=-1)
            r = jax.nn.sigmoid(pre_a + ba_ref[d:d + 1, :])
            gi = jax.nn.sigmoid(pre_i + bi_ref[d:d + 1, :])
            lam = lam_ref[d:d + 1, :]
            log_sig = jnp.minimum(lam, 0.0) - jnp.log(1.0 + jnp.exp(-jnp.abs(lam)))
            log_a = LRU_C * r * log_sig
            a = jnp.exp(log_a)
            bcoef = jnp.sqrt(jnp.maximum(1.0 - jnp.exp(2.0 * log_a), 0.0)) * (gi * u)
            a_ref[pl.ds(r0, tm), :] = a
            b_ref[pl.ds(r0, tm), :] = bcoef
        return carry

    lax.fori_loop(0, t_len // tm, gates, 0)

    def scan(gi, carry):
        hf, hb = carry
        base = pl.multiple_of(gi * 8, 8)
        bbase = jnp.where(base < n_ctx, n_ctx - 1 - base, t_len + n_ctx - 1 - base)
        for j in range(8):
            tf = base + j
            tb = bbase - j
            hf = af_ref[pl.ds(tf, 1), :] * hf + bf_ref[pl.ds(tf, 1), :]
            af_ref[pl.ds(tf, 1), :] = hf
            hb = ab_ref[pl.ds(tb, 1), :] * hb + bb_ref[pl.ds(tb, 1), :]
            ab_ref[pl.ds(tb, 1), :] = hb
        return hf, hb

    zero = jnp.zeros((1, width), F32)
    lax.fori_loop(0, t_len // 8, scan, (zero, zero))
    o_ref[...] = jax.nn.gelu(y_ref[...], approximate=True) * (af_ref[...] + ab_ref[...])


def _lru(u, p, lru_w_a, lru_b_a, lru_w_i, lru_b_i, lru_lam, l, n_ctx):
    b, t, _ = u.shape
    w = LRU_LANES
    nb = w // LRU_BLOCK
    y_off = CONV_CH // w
    wspec = pl.BlockSpec((None, 2, nb, LRU_BLOCK, LRU_BLOCK), lambda bi, j: (l, 0, j, 0, 0))
    vspec = pl.BlockSpec((None, 2, w), lambda bi, j: (l, 0, j))
    return pl.pallas_call(
        functools.partial(_lru_kernel, n_ctx=n_ctx),
        name="lru",
        grid=(b, LRU_WIDTH // w),
        in_specs=[pl.BlockSpec((None, t, w), lambda bi, j: (bi, 0, j)),
                  pl.BlockSpec((None, t, w), lambda bi, j: (bi, 0, y_off + j)),
                  wspec, vspec, wspec, vspec, vspec],
        out_specs=pl.BlockSpec((None, t, w), lambda bi, j: (bi, 0, j)),
        out_shape=jax.ShapeDtypeStruct((b, t, LRU_WIDTH), F32),
        scratch_shapes=[pltpu.VMEM((t, w), F32)] * 4,
        compiler_params=_params("arbitrary", "arbitrary"),
    )(u, p, lru_w_a, lru_b_a, lru_w_i, lru_b_i, lru_lam)


def _split(a):
    hi = a.astype(BF16).astype(F32)
    return hi, a - hi


def _lhs3(a):
    hi, lo = _split(a)
    return jnp.concatenate([hi, lo, hi], axis=1).astype(BF16)


def _rhs3(b):
    hi, lo = _split(b)
    return jnp.concatenate([hi, hi, lo], axis=0).astype(BF16)


def _gdn_kernel(qf_ref, kf_ref, vf_ref, gf_ref, qb_ref, kb_ref, vb_ref, gb_ref, alog_ref, dtb_ref,
                of_ref, ob_ref, s_ref):
    c, h_n, dk, dv = CHUNK, GDN_HEADS, GDN_DK, GDN_DV

    @pl.when(pl.program_id(1) == 0)
    def _():
        s_ref[...] = jnp.zeros_like(s_ref)

    ri = lax.broadcasted_iota(jnp.int32, (c, c), 0)
    ci = lax.broadcasted_iota(jnp.int32, (c, c), 1)
    dirs = ((qf_ref, kf_ref, vf_ref, gf_ref, of_ref), (qb_ref, kb_ref, vb_ref, gb_ref, ob_ref))

    chains = []
    for d, (q_ref, k_ref, v_ref, gt_ref, o_ref) in enumerate(dirs):
        gt = gt_ref[...]
        beta = jax.nn.sigmoid(gt[:, d * h_n:(d + 1) * h_n])
        xa = gt[:, (2 + d) * h_n:(3 + d) * h_n] + dtb_ref[d:d + 1, :]
        softplus = jnp.maximum(xa, 0.0) + jnp.log(1.0 + jnp.exp(-jnp.abs(xa)))
        g = -jnp.exp(alog_ref[d:d + 1, :]) * softplus
        incl = (ri <= ci) if d else (ri >= ci)
        strict = (ri < ci) if d else (ri > ci)
        g_hi, g_rest = _split(g)
        g_mid, g_lo = _split(g_rest)
        g3 = jnp.concatenate([g_hi, g_mid, g_lo], axis=0).astype(BF16)
        ones3 = jnp.concatenate([jnp.where(incl, 1.0, 0.0)] * 3, axis=1).astype(BF16)
        cum = jnp.dot(ones3, g3, preferred_element_type=F32)
        cum_t = lax.dot_general(g3, ones3, (((0,), (1,)), ((), ())),
                                preferred_element_type=F32)
        total = cum[0:1, :] if d else cum[c - 1:c, :]
        e_cum = jnp.exp(cum)
        e_rest = jnp.exp(total - cum)
        e_total = jnp.exp(total)
        for h in range(h_n):
            sl = slice(h * dk, (h + 1) * dk)
            k = k_ref[:, sl]
            kb = k.astype(BF16)
            diff = jnp.where(incl, cum[:, h:h + 1] - cum_t[h:h + 1, :], 0.0)
            gamma = jnp.where(incl, jnp.exp(diff), 0.0)
            bcol = beta[:, h:h + 1]
            kk = lax.dot_general(kb, kb, NT_DIMS, preferred_element_type=F32)
            chains.append(dict(
                d=d, h=h, q_ref=q_ref, k_ref=k_ref, o_ref=o_ref, gamma=gamma,
                neg=jnp.where(strict, -(bcol * kk * gamma), 0.0),
                x=jnp.concatenate([bcol * v_ref[:, sl], bcol * k * e_cum[:, h:h + 1]], axis=-1),
                e_cum=e_cum[:, h:h + 1], e_rest=e_rest[:, h:h + 1], e_total=e_total[:, h:h + 1]))

    for level in range(6):
        for ch in chains:
            a3 = _lhs3(ch["neg"])
            ch["x"] = ch["x"] + jnp.dot(a3, _rhs3(ch["x"]), preferred_element_type=F32)
            if level < 5:
                ch["neg"] = jnp.dot(a3, _rhs3(ch["neg"]), preferred_element_type=F32)

    for ch in chains:
        d, h = ch["d"], ch["h"]
        sl = slice(h * dk, (h + 1) * dk)
        q = ch["q_ref"][:, sl]
        k = ch["k_ref"][:, sl]
        u = ch["x"][:, :dv]
        w = ch["x"][:, dv:]
        qk = lax.dot_general(q.astype(BF16), k.astype(BF16), NT_DIMS,
                             preferred_element_type=F32) * ch["gamma"]
        q_dec = q * ch["e_cum"]
        k_dec = k * ch["e_rest"]
        s = s_ref[d, h]
        ws = jnp.dot(jnp.concatenate([w, q_dec], axis=0).astype(BF16), s.astype(BF16),
                     preferred_element_type=F32)
        v_new = u - ws[:c]
        vb = v_new.astype(BF16)
        ch["o_ref"][:, h * dv:(h + 1) * dv] = ws[c:] + jnp.dot(qk.astype(BF16), vb,
                                                               preferred_element_type=F32)
        s_ref[d, h] = ch["e_total"] * s + lax.dot_general(
            k_dec.astype(BF16), vb, TN_DIMS, preferred_element_type=F32)


def _gdn(u, pg, gdn_a_log, gdn_dt_bias, l, n_ctx):
    b, t, _ = u.shape
    nc = t // CHUNK
    ncc = n_ctx // CHUNK

    def rev(s):
        return jnp.where(s < ncc, ncc - 1 - s, nc + ncc - 1 - s)

    def specs(chunk):
        qkv = [pl.BlockSpec((None, CHUNK, QK_WIDTH), lambda bi, s, col=col: (bi, chunk(s), col))
               for col in (1, 2, 3)]
        return qkv + [pl.BlockSpec((None, CHUNK, N_GATE), lambda bi, s: (bi, chunk(s), 0))]

    fwd = lambda s: s
    small = pl.BlockSpec((None, 2, GDN_HEADS), lambda bi, s: (l, 0, 0))
    out = jax.ShapeDtypeStruct((b, t, V_WIDTH), F32)
    return pl.pallas_call(
        _gdn_kernel,
        name="gdn",
        grid=(b, nc),
        in_specs=specs(fwd) + specs(rev) + [small, small],
        out_specs=[pl.BlockSpec((None, CHUNK, V_WIDTH), lambda bi, s: (bi, s, 0)),
                   pl.BlockSpec((None, CHUNK, V_WIDTH), lambda bi, s: (bi, rev(s), 0))],
        out_shape=[out, out],
        scratch_shapes=[pltpu.VMEM((2, GDN_HEADS, GDN_DK, GDN_DV), F32)],
        compiler_params=_params("arbitrary", "arbitrary"),
    )(u, u, u, pg, u, u, u, pg, gdn_a_log, gdn_dt_bias)


def _outproj_kernel(ylru_ref, of_ref, ob_ref, z_ref, x_ref, mod_ref, gn_ref, w_ref, o_ref):
    o = of_ref[...] + ob_ref[...]
    z = z_ref[...]
    parts = [ylru_ref[...]]
    for h in range(GDN_HEADS):
        sl = slice(h * GDN_DV, (h + 1) * GDN_DV)
        oh = o[:, sl]
        ms = jnp.mean(oh * oh, axis=-1, keepdims=True)
        parts.append(oh * lax.rsqrt(ms + NORM_EPS) * gn_ref[...] * _silu(z[:, sl]))
    feats = jnp.concatenate(parts, axis=-1).astype(BF16)
    y = jnp.dot(feats, w_ref[...], preferred_element_type=F32)
    o_ref[...] = x_ref[...] + mod_ref[2:3, :] * y


def _outproj(y_lru, o_f, o_b, p, x, mods, gdn_norm_g, w_out_bf, l, n_ctx):
    b, t, d = x.shape
    tm = ROW_TILE
    nct = n_ctx // tm
    z_off = (CONV_CH + LRU_WIDTH) // V_WIDTH
    row = lambda bi, i: (bi, i, 0)
    return pl.pallas_call(
        _outproj_kernel,
        name="outproj",
        grid=(b, t // tm),
        in_specs=[pl.BlockSpec((None, tm, LRU_WIDTH), row),
                  pl.BlockSpec((None, tm, V_WIDTH), row),
                  pl.BlockSpec((None, tm, V_WIDTH), row),
                  pl.BlockSpec((None, tm, V_WIDTH), lambda bi, i: (bi, i, z_off)),
                  pl.BlockSpec((None, tm, d), row),
                  pl.BlockSpec((None, None, N_MOD, d),
                               lambda bi, i: (l, jnp.where(i < nct, b, bi), 0, 0)),
                  pl.BlockSpec((None, 1, GDN_DV), lambda bi, i: (l, 0, 0)),
                  pl.BlockSpec((None, LRU_WIDTH + V_WIDTH, d), lambda bi, i: (l, 0, 0))],
        out_specs=pl.BlockSpec((None, tm, d), row),
        out_shape=jax.ShapeDtypeStruct((b, t, d), F32),
        compiler_params=_params("arbitrary", "arbitrary"),
    )(y_lru, o_f, o_b, p, x, mods, gdn_norm_g, w_out_bf)


def _router_kernel(x_ref, mod_ref, g_ref, wr_ref, rb_ref, h_ref, idx_ref, wt_ref, rank_ref,
                   cnt_ref, run_ref, blk_ref):
    first = jnp.logical_and(pl.program_id(0) == 0, pl.program_id(1) == 0)

    @pl.when(first)
    def _():
        run_ref[...] = jnp.zeros_like(run_ref)

    h = _norm_mod(x_ref[...], g_ref[...], mod_ref[3:4, :], mod_ref[4:5, :])
    scores = jax.nn.sigmoid(jnp.dot(h, wr_ref[...], precision=HI, preferred_element_type=F32))
    tm, n_e = scores.shape
    _to_blocks(h, blk_ref)
    h_ref[...] = blk_ref[...].astype(BF16)
    lane = lax.broadcasted_iota(jnp.int32, (tm, n_e), 1)
    kl = lax.broadcasted_iota(jnp.int32, (tm, TOP_K), 1)
    biased = scores + rb_ref[...]
    idx = jnp.zeros((tm, TOP_K), jnp.int32)
    sel = jnp.zeros((tm, TOP_K), F32)
    maskf = jnp.zeros((tm, n_e), F32)
    onehots = []
    for k in range(TOP_K):
        m = jnp.max(biased, axis=-1, keepdims=True)
        ik = jnp.min(jnp.where(biased == m, lane, n_e), axis=-1, keepdims=True)
        oh = lane == ik
        onehots.append(oh)
        sk = jnp.sum(jnp.where(oh, scores, 0.0), axis=-1, keepdims=True)
        idx = jnp.where(kl == k, ik, idx)
        sel = jnp.where(kl == k, sk, sel)
        maskf = jnp.where(oh, 1.0, maskf)
        biased = jnp.where(oh, -jnp.inf, biased)
    wt_ref[...] = sel / jnp.sum(sel, axis=-1, keepdims=True) * ROUTED_SCALE
    idx_ref[...] = idx

    ri = lax.broadcasted_iota(jnp.int32, (tm, tm), 0)
    ci = lax.broadcasted_iota(jnp.int32, (tm, tm), 1)
    before = jnp.where(ri > ci, 1.0, 0.0).astype(BF16)
    prior = jnp.dot(before, maskf.astype(BF16), preferred_element_type=F32) + run_ref[...]
    rank = jnp.zeros((tm, TOP_K), F32)
    for k in range(TOP_K):
        rk = jnp.sum(jnp.where(onehots[k], prior, 0.0), axis=-1, keepdims=True)
        rank = jnp.where(kl == k, rk, rank)
    rank_ref[...] = rank.astype(jnp.int32)
    run_ref[...] = run_ref[...] + jnp.sum(maskf, axis=0, keepdims=True)
    cnt_ref[...] = run_ref[...]


def _router(x, mods, g, w_router, router_bias, l, n_ctx):
    b, t, d = x.shape
    n_e = w_router.shape[-1]
    tm = ROW_TILE
    nt = t // tm
    nct = n_ctx // tm
    nb = d // LANE
    row = lambda bi, i: (bi, i, 0)
    return pl.pallas_call(
        _router_kernel,
        name="router",
        grid=(b, nt),
        in_specs=[pl.BlockSpec((None, tm, d), row),
                  pl.BlockSpec((None, None, N_MOD, d),
                               lambda bi, i: (l, jnp.where(i < nct, b, bi), 0, 0)),
                  pl.BlockSpec((None, 1, d), lambda bi, i: (l, 0, 0)),
                  pl.BlockSpec((None, d, n_e), lambda bi, i: (l, 0, 0)),
                  pl.BlockSpec((None, 1, n_e), lambda bi, i: (l, 0, 0))],
        out_specs=[pl.BlockSpec((tm * nb, LANE), lambda bi, i: (bi * nt + i, 0)),
                   pl.BlockSpec((None, tm, TOP_K), row),
                   pl.BlockSpec((None, tm, TOP_K), row),
                   pl.BlockSpec((None, tm, TOP_K), row),
                   pl.BlockSpec((1, n_e), lambda bi, i: (0, 0))],
        out_shape=[jax.ShapeDtypeStruct((b * t * nb, LANE), BF16),
                   jax.ShapeDtypeStruct((b, t, TOP_K), jnp.int32),
                   jax.ShapeDtypeStruct((b, t, TOP_K), F32),
                   jax.ShapeDtypeStruct((b, t, TOP_K), jnp.int32),
                   jax.ShapeDtypeStruct((1, n_e), F32)],
        scratch_shapes=[pltpu.VMEM((1, n_e), F32), pltpu.VMEM((tm * nb, LANE), F32)],
        compiler_params=_params("arbitrary", "arbitrary"),
    )(x, mods, g, w_router, router_bias)


DISPATCH_TOKENS = 256


def _dispatch_kernel(dest_ref, h_ref, xs_in_ref, xs_ref, sem, *, nb):
    del xs_in_ref

    def issue(r, carry):
        src = h_ref.at[pl.ds(pl.multiple_of(r * nb, nb), nb)]
        for k in range(TOP_K):
            slot = dest_ref[r * TOP_K + k]
            pltpu.make_async_copy(src, xs_ref.at[pl.ds(pl.multiple_of(slot * nb, nb), nb)],
                                  sem).start()
        return carry

    lax.fori_loop(0, DISPATCH_TOKENS, issue, 0)
    for k in range(TOP_K):
        pltpu.make_async_copy(h_ref, xs_ref.at[pl.ds(0, DISPATCH_TOKENS * nb)], sem).wait()


def _dispatch(h_blocks, dest_flat, n_slots, nb):
    n = h_blocks.shape[0] // nb
    xs0 = jnp.zeros((n_slots * nb, LANE), h_blocks.dtype)
    return pl.pallas_call(
        functools.partial(_dispatch_kernel, nb=nb),
        name="dispatch",
        grid=(n // DISPATCH_TOKENS,),
        in_specs=[pl.BlockSpec((DISPATCH_TOKENS * TOP_K,), lambda i: (i,),
                               memory_space=pltpu.SMEM),
                  pl.BlockSpec((DISPATCH_TOKENS * nb, LANE), lambda i: (i, 0)),
                  pl.BlockSpec(memory_space=pl.ANY)],
        out_specs=pl.BlockSpec(memory_space=pl.ANY),
        out_shape=jax.ShapeDtypeStruct((n_slots * nb, LANE), h_blocks.dtype),
        scratch_shapes=[pltpu.SemaphoreType.DMA(())],
        input_output_aliases={2: 0},
        compiler_params=_params("arbitrary"),
    )(dest_flat, h_blocks, xs0)


EXPERT_TILE = 256


def _expert_kernel(te_ref, na_ref, slot_ref, next_ref, x_ref, wg_hbm, wu_hbm, wd_hbm, y_ref,
                   wgf_ref, wuf_ref, wdf_ref, wgb_ref, wub_ref, wdb_ref, blk_ref, wsem, *, layer):
    i = pl.program_id(0)
    active = i < na_ref[0]
    changed = jnp.logical_or(i == 0, te_ref[i] != te_ref[jnp.maximum(i - 1, 0)])
    slot = slot_ref[i]

    def weight_copies(e, s):
        return [pltpu.make_async_copy(hbm.at[layer, e], buf.at[s], wsem.at[s])
                for hbm, buf in ((wg_hbm, wgf_ref), (wu_hbm, wuf_ref), (wd_hbm, wdf_ref))]

    @pl.when(jnp.logical_and(active, i == 0))
    def _():
        for cp in weight_copies(te_ref[0], 0):
            cp.start()

    @pl.when(jnp.logical_and(active, changed))
    def _():
        for cp in weight_copies(te_ref[i], slot):
            cp.wait()

        @pl.when(next_ref[i] >= 0)
        def _():
            for cp in weight_copies(next_ref[i], 1 - slot):
                cp.start()

        wgb_ref[...] = wgf_ref[slot].astype(BF16)
        wub_ref[...] = wuf_ref[slot].astype(BF16)
        wdb_ref[...] = wdf_ref[slot].astype(BF16)

    @pl.when(active)
    def _():
        blk_ref[...] = x_ref[...].astype(F32)
        xb = _from_blocks(blk_ref, EXPERT_TILE).astype(BF16)
        hid = _silu(jnp.dot(xb, wgb_ref[...], preferred_element_type=F32)) * jnp.dot(
            xb, wub_ref[...], preferred_element_type=F32)
        y_ref[...] = jnp.dot(hid.astype(BF16), wdb_ref[...], preferred_element_type=F32)

    @pl.when(jnp.logical_not(active))
    def _():
        y_ref[...] = jnp.zeros_like(y_ref)


def _experts(xs, tile_expert, n_active, weight_slot, next_expert, w_gate, w_up, w_down, l):
    d, f = w_gate.shape[-2:]
    nb = d // LANE
    n_slots = xs.shape[0] // nb
    tm = EXPERT_TILE
    hbm = pl.BlockSpec(memory_space=pl.ANY)
    grid_spec = pltpu.PrefetchScalarGridSpec(
        num_scalar_prefetch=4,
        grid=(n_slots // tm,),
        in_specs=[pl.BlockSpec((tm * nb, LANE), lambda i, *_: (i, 0)), hbm, hbm, hbm],
        out_specs=pl.BlockSpec((tm, d), lambda i, *_: (i, 0)),
        scratch_shapes=[pltpu.VMEM((2, d, f), F32), pltpu.VMEM((2, d, f), F32),
                        pltpu.VMEM((2, f, d), F32),
                        pltpu.VMEM((d, f), BF16), pltpu.VMEM((d, f), BF16),
                        pltpu.VMEM((f, d), BF16), pltpu.VMEM((tm * nb, LANE), F32),
                        pltpu.SemaphoreType.DMA((2,))],
    )
    return pl.pallas_call(
        functools.partial(_expert_kernel, layer=l),
        name="experts",
        grid_spec=grid_spec,
        out_shape=jax.ShapeDtypeStruct((n_slots, d), F32),
        compiler_params=_params("arbitrary"),
    )(tile_expert, n_active, weight_slot, next_expert, xs, w_gate, w_up, w_down)


COMBINE_TOKENS = 128


def _combine_kernel(dest_ref, ys_ref, x_ref, g_ref, wt_ref, mod_ref, sg_ref, su_ref, sd_ref,
                    o_ref, buf_ref, sem):
    tm = x_ref.shape[0]

    def issue(r, carry):
        for k in range(TOP_K):
            slot = dest_ref[r * TOP_K + k]
            pltpu.make_async_copy(ys_ref.at[pl.ds(slot, 1)], buf_ref.at[pl.ds(k * tm + r, 1)],
                                  sem).start()
        return carry

    lax.fori_loop(0, tm, issue, 0)
    x = x_ref[...]
    hb = _norm_mod(x, g_ref[...], mod_ref[3:4, :], mod_ref[4:5, :]).astype(BF16)
    hid = _silu(jnp.dot(hb, sg_ref[...], preferred_element_type=F32)) * jnp.dot(
        hb, su_ref[...], preferred_element_type=F32)
    y = jnp.dot(hid.astype(BF16), sd_ref[...], preferred_element_type=F32)
    pltpu.make_async_copy(ys_ref.at[pl.ds(0, TOP_K * tm)], buf_ref, sem).wait()
    wt = wt_ref[...]
    for k in range(TOP_K):
        y = y + wt[:, k:k + 1] * buf_ref[pl.ds(k * tm, tm), :]
    o_ref[...] = x + mod_ref[5:6, :] * y


def _combine(ys, dest_flat, wts, x, g, mods, ws_gate_bf, ws_up_bf, ws_down_bf, l, n_ctx):
    b, t, d = x.shape
    tm = COMBINE_TOKENS
    nt = t // tm
    nct = n_ctx // tm
    fs = ws_gate_bf.shape[-1]
    row = lambda bi, i: (bi, i, 0)
    return pl.pallas_call(
        _combine_kernel,
        name="combine",
        grid=(b, nt),
        in_specs=[pl.BlockSpec((tm * TOP_K,), lambda bi, i: (bi * nt + i,),
                               memory_space=pltpu.SMEM),
                  pl.BlockSpec(memory_space=pl.ANY),
                  pl.BlockSpec((None, tm, d), row),
                  pl.BlockSpec((None, 1, d), lambda bi, i: (l, 0, 0)),
                  pl.BlockSpec((None, tm, TOP_K), row),
                  pl.BlockSpec((None, None, N_MOD, d),
                               lambda bi, i: (l, jnp.where(i < nct, b, bi), 0, 0)),
                  pl.BlockSpec((None, d, fs), lambda bi, i: (l, 0, 0)),
                  pl.BlockSpec((None, d, fs), lambda bi, i: (l, 0, 0)),
                  pl.BlockSpec((None, fs, d), lambda bi, i: (l, 0, 0))],
        out_specs=pl.BlockSpec((None, tm, d), row),
        out_shape=jax.ShapeDtypeStruct((b, t, d), F32),
        scratch_shapes=[pltpu.VMEM((TOP_K * tm, d), F32), pltpu.SemaphoreType.DMA(())],
        compiler_params=_params("arbitrary", "arbitrary"),
    )(dest_flat, ys, x, g, wts, mods, ws_gate_bf, ws_up_bf, ws_down_bf)


def _final_kernel(x_ref, g_ref, o_ref):
    x = x_ref[...]
    ms = jnp.mean(x * x, axis=-1, keepdims=True)
    o_ref[...] = x * lax.rsqrt(ms + NORM_EPS) * g_ref[...]


def _final_norm(x, g_final, n_ctx):
    b, t, d = x.shape
    tm = ROW_TILE
    nct = n_ctx // tm
    return pl.pallas_call(
        _final_kernel,
        name="final_norm",
        grid=(b, (t - n_ctx) // tm),
        in_specs=[pl.BlockSpec((None, tm, d), lambda bi, i: (bi, i + nct, 0)),
                  pl.BlockSpec((1, d), lambda bi, i: (0, 0))],
        out_specs=pl.BlockSpec((None, tm, d), lambda bi, i: (bi, i, 0)),
        out_shape=jax.ShapeDtypeStruct((b, t - n_ctx, d), F32),
        compiler_params=_params("arbitrary", "arbitrary"),
    )(x, g_final.reshape(1, d))


def _latent_permute(xs, n_ctx, to_col_major):
    b, t, d = xs.shape
    rows = (t - n_ctx) // GRID_W
    lat = xs[:, n_ctx:]
    if to_col_major:
        lat = lat.reshape(b, rows, GRID_W, d).transpose(0, 2, 1, 3)
    else:
        lat = lat.reshape(b, GRID_W, rows, d).transpose(0, 2, 1, 3)
    return jnp.concatenate([xs[:, :n_ctx], lat.reshape(b, t - n_ctx, d)], axis=1)


def _routing_plan(idx, rank, counts, n_tiles):
    n_e = counts.shape[-1]
    cnt = counts.reshape(n_e).astype(jnp.int32)
    tiles = (cnt + EXPERT_TILE - 1) // EXPERT_TILE
    tile_end = jnp.cumsum(tiles)
    start = (tile_end - tiles) * EXPERT_TILE
    onehot = idx[..., None] == jnp.arange(n_e, dtype=jnp.int32)
    dest = rank + jnp.sum(jnp.where(onehot, start, 0), axis=-1)
    tile_ids = jnp.arange(n_tiles, dtype=jnp.int32)
    tile_expert = jnp.sum(tile_ids[:, None] >= tile_end[None, :], axis=-1).astype(jnp.int32)
    n_active = tile_end[-1]
    tile_expert = jnp.where(tile_ids < n_active, jnp.minimum(tile_expert, n_e - 1),
                            jnp.sum(jnp.where(tile_ids == n_active - 1, tile_expert, 0)))
    changed = jnp.concatenate([jnp.ones((1,), bool), tile_expert[1:] != tile_expert[:-1]])
    weight_slot = ((jnp.cumsum(changed) - 1) % 2).astype(jnp.int32)
    next_change = lax.cummin(jnp.where(changed, tile_ids, n_tiles), axis=0, reverse=True)
    next_change = jnp.concatenate([next_change[1:], jnp.full((1,), n_tiles, jnp.int32)])
    next_expert = jnp.where(next_change < n_tiles,
                            tile_expert[jnp.minimum(next_change, n_tiles - 1)], -1)
    return (dest.reshape(-1).astype(jnp.int32), tile_expert.astype(jnp.int32),
            n_active.reshape(1).astype(jnp.int32), weight_slot, next_expert.astype(jnp.int32))


def kernel(x, c, ctx, c_ctx, w_mod, b_mod, g_mix, g_ffn, w_in, conv_w, lru_w_a, lru_b_a, lru_w_i,
           lru_b_i, lru_lam, gdn_a_log, gdn_dt_bias, gdn_norm_g, w_out, w_router, router_bias,
           w_gate, w_up, w_down, ws_gate, ws_up, ws_down, g_final):
    b, seq, d = x.shape
    n_ctx = ctx.shape[1]
    t = n_ctx + seq
    depth = w_mod.shape[0]
    n_e = w_router.shape[-1]
    assert n_ctx % ROW_TILE == 0 and seq % ROW_TILE == 0 and b < 8
    n_tiles = (b * t * TOP_K) // EXPERT_TILE + n_e

    mods = _modulation(c, c_ctx, w_mod, b_mod)
    g_mix3 = g_mix.reshape(depth, 1, d)
    g_ffn3 = g_ffn.reshape(depth, 1, d)
    gn3 = gdn_norm_g.reshape(depth, 1, GDN_DV)
    rb3 = router_bias.reshape(depth, 1, n_e)
    w_out_bf = w_out.astype(BF16)
    ws_gate_bf, ws_up_bf, ws_down_bf = (w.astype(BF16) for w in (ws_gate, ws_up, ws_down))

    xs = jnp.concatenate([ctx, x], axis=1)
    for l in range(depth):
        col_major = l % 2 == 1
        if col_major:
            xs = _latent_permute(xs, n_ctx, True)
        h, pg = _inprep(xs, mods, g_mix3, w_in[l, :, MAIN_WIDTH:].astype(BF16), l, n_ctx)
        p = _inproj(h.reshape(b * t, d), w_in, l).reshape(b, t, MAIN_WIDTH)
        u = _conv(p, conv_w, l, n_ctx)
        y_lru = _lru(u, p, lru_w_a, lru_b_a, lru_w_i, lru_b_i, lru_lam, l, n_ctx)
        o_f, o_b = _gdn(u, pg, gdn_a_log, gdn_dt_bias, l, n_ctx)
        xs = _outproj(y_lru, o_f, o_b, p, xs, mods, gn3, w_out_bf, l, n_ctx)
        if col_major:
            xs = _latent_permute(xs, n_ctx, False)

        h2, idx, wts, rank, counts = _router(xs, mods, g_ffn3, w_router, rb3, l, n_ctx)
        dest, tile_expert, n_active, weight_slot, next_expert = _routing_plan(idx, rank, counts,
                                                                              n_tiles)
        xsorted = _dispatch(h2, dest, n_tiles * EXPERT_TILE, d // LANE)
        ysorted = _experts(xsorted, tile_expert, n_active, weight_slot, next_expert, w_gate, w_up,
                           w_down, l)
        xs = _combine(ysorted, dest, wts, xs, g_ffn3, mods, ws_gate_bf, ws_up_bf, ws_down_bf, l,
                      n_ctx)
    return _final_norm(xs, g_final, n_ctx)
```
